```python
import math
import jax, jax.numpy as jnp
from jax import lax
import numpy as np

D_MODEL = 1024
BATCH = 8
SEQ = 4096
DEPTH = 4

CHUNK = 64
Q_BLOCK = 128
N_A_LAYERS = DEPTH // 2
N_B_LAYERS = DEPTH - N_A_LAYERS
RWKV_HEAD = 64
RWKV_HEADS = D_MODEL // RWKV_HEAD
DECAY_LORA = 64
ICLR_LORA = 64
GATE_LORA = 160
LNX_EPS = 64e-5
MLA_HEADS = 16
QK_NOPE = 64
QK_ROPE = 32
V_HEAD = 64
Q_LORA = 384
KV_LORA = 256
ROPE_THETA = 10000.0
NEG_INF = -1e30
D_FF = 2816
CONV_W = 3
NORM_EPS = 1e-6

kernel_name = "rwkv7_yoco_mla_convffn_trunk"


def rmsnorm(x, g):
    xf = x.astype(jnp.float32)
    y = xf * lax.rsqrt(jnp.mean(xf * xf, axis=-1, keepdims=True) + NORM_EPS)
    return (y * g.astype(jnp.float32)).astype(x.dtype)


def conv_ffn(x, w_in, conv_w, conv_b, w_out):
    T = x.shape[1]
    gate, up = jnp.split(x @ w_in, 2, axis=-1)
    gp = jnp.pad(gate, ((0, 0), (CONV_W - 1, 0), (0, 0)))
    gc = conv_b
    for j in range(CONV_W):
        gc = gc + gp[:, j:j + T, :] * conv_w[j]
    h = jax.nn.gelu(gc) * up
    return h @ w_out


def _rwkv7_step(S, inp):
    r, w, k, v, a, b = inp
    sa = jnp.einsum('bhvk,bhk->bhv', S, a)
    S = S * w[:, :, None, :] + sa[..., None] * b[:, :, None, :] + v[..., None] * k[:, :, None, :]
    y = jnp.einsum('bhvk,bhk->bhv', S, r)
    return S, y


def rwkv7_time_mix(x, mu, w_rkv, w0, w1, w2, a0, a1, a2, g1, g2, k_k, k_a, r_k, lnx_w, lnx_b, w_o):
    B, T, D = x.shape
    H, N = RWKV_HEADS, RWKV_HEAD
    f32 = jnp.float32
    x_prev = jnp.pad(x, ((0, 0), (1, 0), (0, 0)))[:, :T]
    xx = x_prev - x
    xs = x[None] + xx[None] * mu[:, None, None, :]
    rkv = jnp.einsum('nbtd,nde->nbte', xs[:3], w_rkv)
    r, k, v = rkv[0], rkv[1], rkv[2]
    xw, xa, xg = xs[3], xs[4], xs[5]
    w_log = -jax.nn.softplus(-(w0 + jnp.tanh(xw @ w1) @ w2)) - 0.5
    decay = jnp.exp(-jnp.exp(w_log.astype(f32)))
    a = jax.nn.sigmoid(a0 + (xa @ a1) @ a2)
    g = jax.nn.sigmoid(xg @ g1) @ g2
    kk = (k * k_k).reshape(B, T, H, N).astype(f32)
    kk = kk / jnp.maximum(jnp.sqrt(jnp.sum(kk * kk, axis=-1, keepdims=True)), 1e-12)
    k = k * (1.0 + (a - 1.0) * k_a)
    heads = lambda t: t.reshape(B, T, H, N).astype(f32)
    rh, kh, vh, wh, ah = heads(r), heads(k), heads(v), heads(decay), heads(a)
    seq = tuple(jnp.moveaxis(t, 1, 0) for t in (rh, wh, kh, vh, -kk, kk * ah))
    S0 = jnp.zeros((B, H, N, N), f32)
    _, y = lax.scan(_rwkv7_step, S0, seq)
    y = jnp.moveaxis(y, 0, 1)
    mean = jnp.mean(y, axis=-1, keepdims=True)
    var = jnp.mean(jnp.square(y - mean), axis=-1, keepdims=True)
    yn = ((y - mean) * lax.rsqrt(var + LNX_EPS)).reshape(B, T, D) * lnx_w + lnx_b
    bonus = (jnp.sum(rh * kh * r_k, axis=-1, keepdims=True) * vh).reshape(B, T, D)
    out = ((yn + bonus) * g).astype(x.dtype)
    return out @ w_o


def rope_tables(T):
    inv = 1.0 / (ROPE_THETA ** (jnp.arange(0, QK_ROPE, 2, dtype=jnp.float32) / QK_ROPE))
    ang = jnp.arange(T, dtype=jnp.float32)[:, None] * inv[None, :]
    return jnp.cos(ang), jnp.sin(ang)


def apply_rope(x, cos, sin):
    half = x.shape[-1] // 2
    xf = x.astype(jnp.float32)
    x1, x2 = xf[..., :half], xf[..., half:]
    return jnp.concatenate([x1 * cos - x2 * sin, x1 * sin + x2 * cos], axis=-1).astype(x.dtype)


def mla_attend(q_nope, q_rope, k_nope, k_rope, v):
    B, T, H, _ = q_nope.shape
    nb = T // Q_BLOCK
    scale = 1.0 / math.sqrt(QK_NOPE + QK_ROPE)
    k_chunk = jnp.arange(T) // CHUNK

    def blocks(t):
        return jnp.moveaxis(t.reshape((B, nb, Q_BLOCK) + t.shape[2:]), 1, 0)

    def one_block(args):
        qn, qr, bi = args
        s = jnp.einsum('bqhd,bkhd->bhqk', qn, k_nope) + jnp.einsum('bqhr,bkr->bhqk', qr, k_rope)
        s = s.astype(jnp.float32) * scale
        q_chunk = (bi * Q_BLOCK + jnp.arange(Q_BLOCK)) // CHUNK
        mask = k_chunk[None, :] <= q_chunk[:, None]
        s = jnp.where(mask[None, None], s, NEG_INF)
        p = jax.nn.softmax(s, axis=-1)
        return jnp.einsum('bhqk,bkhd->bqhd', p.astype(v.dtype), v)

    out = lax.map(one_block, (blocks(q_nope), blocks(q_rope), jnp.arange(nb)))
    return jnp.moveaxis(out, 0, 1).reshape(B, T, H, V_HEAD)


def setup_inputs(seed: int = 0) -> dict:
    key = jax.random.key(seed)
    ks = jax.random.split(key, 32)
    D, F, H, N = D_MODEL, D_FF, RWKV_HEADS, RWKV_HEAD
    nA, nB = N_A_LAYERS, N_B_LAYERS
    nrm = lambda k, shape, s: jax.random.normal(k, shape, jnp.float32) * s
    return {
        'x': nrm(ks[0], (BATCH, SEQ, D), 1.0),
        'norm_g': 1.0 + nrm(ks[1], (DEPTH, 4, D), 0.05),
        'ffn_w_in': nrm(ks[2], (DEPTH, D, 2 * F), D ** -0.5),
        'ffn_conv_w': nrm(ks[3], (DEPTH, CONV_W, F), CONV_W ** -0.5),
        'ffn_conv_b': nrm(ks[4], (DEPTH, F), 0.01),
        'ffn_w_out': nrm(ks[5], (DEPTH, F, D), F ** -0.5),
        'a_mu': jax.random.uniform(ks[6], (nA, 6, D), jnp.float32),
        'a_w_rkv': nrm(ks[7], (nA, 3, D, D), D ** -0.5),
        'a_w0': jax.random.uniform(ks[8], (nA, D), jnp.float32, minval=-6.5, maxval=-1.5),
        'a_w1': nrm(ks[9], (nA, D, DECAY_LORA), 0.1 * D ** -0.5),
        'a_w2': nrm(ks[10], (nA, DECAY_LORA, D), DECAY_LORA ** -0.5),
        'a_a0': nrm(ks[11], (nA, D), 0.1),
        'a_a1': nrm(ks[12], (nA, D, ICLR_LORA), 0.1 * D ** -0.5),
        'a_a2': nrm(ks[13], (nA, ICLR_LORA, D), ICLR_LORA ** -0.5),
        'a_g1': nrm(ks[14], (nA, D, GATE_LORA), D ** -0.5),
        'a_g2': nrm(ks[15], (nA, GATE_LORA, D), GATE_LORA ** -0.5),
        'a_k_k': 0.85 + nrm(ks[16], (nA, D), 0.05),
        'a_k_a': 1.0 + nrm(ks[17], (nA, D), 0.05),
        'a_r_k': nrm(ks[18], (nA, H, N), 0.1),
        'a_lnx_w': 1.0 + nrm(ks[19], (nA, D), 0.05),
        'a_lnx_b': nrm(ks[20], (nA, D), 0.01),
        'a_w_o': nrm(ks[21], (nA, D, D), D ** -0.5),
        'kv_norm_g': 1.0 + nrm(ks[22], (D,), 0.05),
        'kv_w_down': nrm(ks[23], (D, KV_LORA + QK_ROPE), D ** -0.5),
        'kv_a_norm_g': 1.0 + nrm(ks[24], (KV_LORA,), 0.05),
        'kv_w_up': nrm(ks[25], (KV_LORA, MLA_HEADS * (QK_NOPE + V_HEAD)), KV_LORA ** -0.5),
        'q_w_down': nrm(ks[26], (nB, D, Q_LORA), D ** -0.5),
        'q_norm_g': 1.0 + nrm(ks[27], (nB, Q_LORA), 0.05),
        'q_w_up': nrm(ks[28], (nB, Q_LORA, MLA_HEADS * (QK_NOPE + QK_ROPE)), Q_LORA ** -0.5),
        'o_w': nrm(ks[29], (nB, MLA_HEADS * V_HEAD, D), (MLA_HEADS * V_HEAD) ** -0.5),
    }


def reference(x, norm_g, ffn_w_in, ffn_conv_w, ffn_conv_b, ffn_w_out,
              a_mu, a_w_rkv, a_w0, a_w1, a_w2, a_a0, a_a1, a_a2, a_g1, a_g2,
              a_k_k, a_k_a, a_r_k, a_lnx_w, a_lnx_b, a_w_o,
              kv_norm_g, kv_w_down, kv_a_norm_g, kv_w_up,
              q_w_down, q_norm_g, q_w_up, o_w):
    B, T, _ = x.shape
    cos, sin = rope_tables(T)
    k_nope = k_rope = v_sh = None
    for layer in range(DEPTH):
        h = rmsnorm(x, norm_g[layer, 0])
        if layer < N_A_LAYERS:
            i = layer
            m = rwkv7_time_mix(h, a_mu[i], a_w_rkv[i], a_w0[i], a_w1[i], a_w2[i],
                               a_a0[i], a_a1[i], a_a2[i], a_g1[i], a_g2[i],
                               a_k_k[i], a_k_a[i], a_r_k[i], a_lnx_w[i], a_lnx_b[i], a_w_o[i])
        else:
            if layer == N_A_LAYERS:
                down = rmsnorm(x, kv_norm_g) @ kv_w_down
                c_kv = rmsnorm(down[..., :KV_LORA], kv_a_norm_g)
                k_rope = apply_rope(down[..., KV_LORA:], cos, sin)
                up = (c_kv @ kv_w_up).reshape(B, T, MLA_HEADS, QK_NOPE + V_HEAD)
                k_nope, v_sh = up[..., :QK_NOPE], up[..., QK_NOPE:]
            j = layer - N_A_LAYERS
            c_q = rmsnorm(h @ q_w_down[j], q_norm_g[j])
            q = (c_q @ q_w_up[j]).reshape(B, T, MLA_HEADS, QK_NOPE + QK_ROPE)
            q_nope = q[..., :QK_NOPE]
            q_rope = apply_rope(q[..., QK_NOPE:], cos[:, None, :], sin[:, None, :])
            o = mla_attend(q_nope, q_rope, k_nope, k_rope, v_sh)
            m = o.reshape(B, T, MLA_HEADS * V_HEAD) @ o_w[j]
        x = x + rmsnorm(m, norm_g[layer, 1])
        f = conv_ffn(rmsnorm(x, norm_g[layer, 2]), ffn_w_in[layer], ffn_conv_w[layer],
                     ffn_conv_b[layer], ffn_w_out[layer])
        x = x + rmsnorm(f, norm_g[layer, 3])
    return x
```

```python
import functools
import math

import jax
import jax.numpy as jnp
from jax import lax
from jax.experimental import pallas as pl
from jax.experimental.pallas import tpu as pltpu

F32 = jnp.float32
BF16 = jnp.bfloat16

HEAD = 64
LNX_EPS = 64e-5
NORM_EPS = 1e-6
QK_NOPE = 64
QK_ROPE = 32
KV_LORA = 256
ROPE_THETA = 10000.0
MASK_CHUNK = 64
NEG_INF = -1e30

LANES = 128
SUBLANES = 8
VMEM_LIMIT = 48 * 1024 * 1024

SCAN_CHUNK = 64
ROW_TILE = 512
FFN_ROW_TILE = 1024
FFN_COL_TILE = 256
ATT_TILE = 512


def _bf(x):
    return x.astype(BF16)


def _mm(a, b):
    return jnp.dot(a, b, preferred_element_type=F32)


def _mm_nt(a, b):
    return lax.dot_general(a, b, (((1,), (1,)), ((), ())), preferred_element_type=F32)


def _mm_tn(a, b):
    return lax.dot_general(a, b, (((0,), (0,)), ((), ())), preferred_element_type=F32)


def _rms(x, g):
    return x * lax.rsqrt(jnp.mean(x * x, axis=-1, keepdims=True) + NORM_EPS) * g


def _sigmoid(x):
    return 1.0 / (1.0 + jnp.exp(-x))


def _params(*sem):
    return pltpu.CompilerParams(dimension_semantics=sem, vmem_limit_bytes=VMEM_LIMIT)


def _full(shape):
    nd = len(shape)
    return pl.BlockSpec(shape, lambda *_: (0,) * nd)


def _rwkv_prep_kernel(seq_len, x_ref, xh_ref, vec_ref, wrkv_ref, w1_ref, w2_ref, a1_ref, a2_ref,
                      g1_ref, g2_ref, e_ref, et_ref,
                      r_ref, k_ref, v_ref, a_ref, b_ref, g_ref, lw_ref):
    tm = x_ref.shape[0]
    gn = vec_ref[0:1, :]
    h = _rms(x_ref[...], gn)
    h_halo = _rms(xh_ref[SUBLANES - 1:SUBLANES, :], gn)
    at_start = (pl.program_id(0) * tm) % seq_len == 0
    h_halo = jnp.where(at_start, 0.0, h_halo)
    row = lax.broadcasted_iota(jnp.int32, h.shape, 0)
    h_prev = jnp.where(row == 0, h_halo, pltpu.roll(h, 1, 0))
    xx = h_prev - h

    def mix(i):
        return _bf(h + xx * vec_ref[1 + i:2 + i, :])

    r = _mm(mix(0), wrkv_ref[0])
    k = _mm(mix(1), wrkv_ref[1])
    v = _mm(mix(2), wrkv_ref[2])
    zw = vec_ref[7:8, :] + _mm(_bf(jnp.tanh(_mm(mix(3), w1_ref[...]))), w2_ref[...])
    softplus = jnp.maximum(-zw, 0.0) + jnp.log(1.0 + jnp.exp(-jnp.abs(zw)))
    lw = -jnp.exp(-softplus - 0.5)
    a = _sigmoid(vec_ref[8:9, :] + _mm(_bf(_mm(mix(4), a1_ref[...])), a2_ref[...]))
    g = _mm(_bf(_sigmoid(_mm(mix(5), g1_ref[...]))), g2_ref[...])
    kk = k * vec_ref[9:10, :]
    ss = _mm(_bf(_mm(_bf(kk * kk), e_ref[...])), et_ref[...])
    kk = kk / jnp.maximum(jnp.sqrt(ss), 1e-12)
    k = k * (1.0 + (a - 1.0) * vec_ref[10:11, :])
    r_ref[...] = _bf(r)
    k_ref[...] = _bf(k)
    v_ref[...] = _bf(v)
    a_ref[...] = _bf(-kk)
    b_ref[...] = _bf(kk * a)
    g_ref[...] = _bf(g)
    lw_ref[...] = lw


def _rwkv_prep(x, seq_len, vec, wrkv, w1, w2, a1, a2, g1, g2, e, et):
    m, d = x.shape
    tm = ROW_TILE
    halo = tm // SUBLANES
    row_spec = pl.BlockSpec((tm, d), lambda i: (i, 0))
    out_bf = jax.ShapeDtypeStruct((m, d), BF16)
    return pl.pallas_call(
        functools.partial(_rwkv_prep_kernel, seq_len),
        grid=(m // tm,),
        in_specs=[row_spec,
                  pl.BlockSpec((SUBLANES, d), lambda i: (jnp.maximum(i * halo - 1, 0), 0)),
                  _full(vec.shape), _full(wrkv.shape), _full(w1.shape), _full(w2.shape),
                  _full(a1.shape), _full(a2.shape), _full(g1.shape), _full(g2.shape),
                  _full(e.shape), _full(et.shape)],
        out_specs=[row_spec] * 7,
        out_shape=[out_bf] * 6 + [jax.ShapeDtypeStruct((m, d), F32)],
        compiler_params=_params("parallel"),
        name="rwkv_prep",
    )(x, x, vec, wrkv, w1, w2, a1, a2, g1, g2, e, et)


GROUP = 4
GROUP_W = GROUP * HEAD
PAIR_W = 2 * HEAD


def _unit_lower_inverse(nm, row, col):
    diff = row ^ col
    eye = jnp.where(row == col, 1.0, 0.0)
    n8 = jnp.where((diff >> 3) == 0, nm, 0.0)
    n8b = _bf(n8)
    n8_2 = _mm(n8b, n8b)
    n8_2b = _bf(n8_2)
    n8_4 = _mm(n8_2b, n8_2b)
    x = eye + n8 + n8_2 + _mm(n8b, n8_2b)
    x = x + _mm(_bf(x), _bf(n8_4))
    for shift in (3, 4, 5):
        off = jnp.where((diff >> shift) == 1, nm, 0.0)
        xb = _bf(x)
        x = x + _mm(_bf(_mm(xb, _bf(off))), xb)
    return x


def _rwkv_scan_kernel(r_ref, k_ref, v_ref, a_ref, b_ref, g_ref, lw_ref, vec_ref, bd_ref,
                      o_ref, s_ref):
    chunk, d = lw_ref.shape

    @pl.when(pl.program_id(1) == 0)
    def _():
        s_ref[...] = jnp.zeros_like(s_ref)

    lw = lw_ref[...]
    trow = lax.broadcasted_iota(jnp.int32, (chunk, chunk), 0)
    tcol = lax.broadcasted_iota(jnp.int32, (chunk, chunk), 1)
    tri = _bf(jnp.where(trow >= tcol, 1.0, 0.0))
    lw_hi = _bf(lw)
    lw_lo = _bf(lw - lw_hi.astype(F32))
    c = _mm(tri, lw_hi) + _mm(tri, lw_lo)
    c_last = c[chunk - 1:chunk, :]
    r = r_ref[...].astype(F32)
    k = k_ref[...].astype(F32)
    a = a_ref[...].astype(F32)
    b = b_ref[...].astype(F32)
    e_inv = jnp.exp(-c)
    e_rem = jnp.exp(c_last - c)
    at = a * jnp.exp(c - lw)
    rt = r * jnp.exp(c)
    bt = b * e_inv
    kt = k * e_inv
    bh = _bf(b * e_rem)
    kh = _bf(k * e_rem)
    g_last = jnp.exp(c_last)
    atb = _bf(at)
    rtb = _bf(rt)

    lane_g = lax.broadcasted_iota(jnp.int32, (chunk, GROUP_W), 1) >> 6
    row = lax.broadcasted_iota(jnp.int32, (GROUP_W, GROUP_W), 0)
    col = lax.broadcasted_iota(jnp.int32, (GROUP_W, GROUP_W), 1)
    strict = (row & (HEAD - 1)) > (col & (HEAD - 1))
    incl = (row & (HEAD - 1)) >= (col & (HEAD - 1))
    first_half = lax.broadcasted_iota(jnp.int32, (chunk, PAIR_W), 1) < HEAD
    prow = lax.broadcasted_iota(jnp.int32, (PAIR_W, PAIR_W), 0)
    pcol = lax.broadcasted_iota(jnp.int32, (PAIR_W, PAIR_W), 1)
    same_head = (prow >> 6) == (pcol >> 6)
    bd = bd_ref[...]

    def stack(x, sl):
        xs = x[:, sl]
        return jnp.concatenate([_bf(jnp.where(lane_g == h, xs, 0.0)) for h in range(GROUP)], axis=0)

    for q in range(d // GROUP_W):
        gsl = slice(q * GROUP_W, (q + 1) * GROUP_W)
        lhs = jnp.concatenate([stack(at, gsl), stack(rt, gsl)], axis=0)
        rhs = jnp.concatenate([stack(bt, gsl), stack(kt, gsl)], axis=0)
        p = _mm_nt(lhs, rhs)
        n_ab = jnp.where(strict, p[:GROUP_W, :GROUP_W], 0.0)
        a_ak = jnp.where(strict, p[:GROUP_W, GROUP_W:], 0.0)
        a_rb = jnp.where(incl, p[GROUP_W:, :GROUP_W], 0.0)
        a_rk = jnp.where(incl, p[GROUP_W:, GROUP_W:], 0.0)
        tinv = _bf(_unit_lower_inverse(n_ab, row, col))
        tt = jnp.concatenate([tinv, _bf(_mm(tinv, _bf(a_ak)))], axis=1)
        aa = jnp.concatenate([_bf(a_rb), _bf(a_rk)], axis=1)

        pairs = (2 * q, 2 * q + 1)
        qa, qr, vp, s0 = [], [], [], []
        for p_i in pairs:
            psl = slice(p_i * PAIR_W, (p_i + 1) * PAIR_W)
            s = s_ref[p_i]
            qq = _mm_nt(jnp.concatenate([atb[:, psl], rtb[:, psl]], axis=0), _bf(s))
            qa.append(_bf(qq[:chunk]))
            qr.append(qq[chunk:])
            vp.append(v_ref[:, psl])
            s0.append(s)
        v_st = jnp.concatenate([vp[0], vp[0], vp[1], vp[1]], axis=0)
        qa_st = jnp.concatenate([qa[0], qa[0], qa[1], qa[1]], axis=0)
        qr_st = jnp.concatenate([qr[0], qr[0], qr[1], qr[1]], axis=0)
        u_st = _mm(tt, jnp.concatenate([qa_st, v_st], axis=0))
        y_st = qr_st + _mm(aa, jnp.concatenate([_bf(u_st), v_st], axis=0))

        for j, p_i in enumerate(pairs):
            psl = slice(p_i * PAIR_W, (p_i + 1) * PAIR_W)
            lo, mid, hi = 2 * j * chunk, (2 * j + 1) * chunk, (2 * j + 2) * chunk
            u = jnp.where(first_half, u_st[lo:mid], u_st[mid:hi])
            y = jnp.where(first_half, y_st[lo:mid], y_st[mid:hi])
            ds = _mm_tn(jnp.concatenate([_bf(u), vp[j]], axis=0),
                        jnp.concatenate([bh[:, psl], kh[:, psl]], axis=0))
            s_ref[p_i] = s0[j] * g_last[:, psl] + jnp.where(same_head, ds, 0.0)

            mean = _mm(_bf(y), bd) * (1.0 / HEAD)
            dev = y - mean
            var = _mm(_bf(dev * dev), bd) * (1.0 / HEAD)
            yn = dev * lax.rsqrt(var + LNX_EPS) * vec_ref[0:1, psl] + vec_ref[1:2, psl]
            rk = r[:, psl] * k[:, psl] * vec_ref[2:3, psl]
            bonus = _mm(_bf(rk), bd) * vp[j].astype(F32)
            o_ref[:, psl] = _bf((yn + bonus) * g_ref[:, psl].astype(F32))


def _rwkv_scan(r, k, v, a, b, g, lw, vec, bd, batch, seq_len):
    m, d = lw.shape
    nc = seq_len // SCAN_CHUNK
    blk = pl.BlockSpec((SCAN_CHUNK, d), lambda bi, ci: (bi * nc + ci, 0))
    return pl.pallas_call(
        _rwkv_scan_kernel,
        grid=(batch, nc),
        in_specs=[blk] * 7 + [_full(vec.shape), _full(bd.shape)],
        out_specs=blk,
        out_shape=jax.ShapeDtypeStruct((m, d), BF16),
        scratch_shapes=[pltpu.VMEM((d // PAIR_W, PAIR_W, PAIR_W), F32)],
        compiler_params=_params("parallel", "arbitrary"),
        name="rwkv_scan",
    )(r, k, v, a, b, g, lw, vec, bd)


def _proj_residual_kernel(x_ref, o_ref, w_ref, g_ref, out_ref):
    out_ref[...] = x_ref[...] + _rms(_mm(o_ref[...], w_ref[...]), g_ref[...])


def _proj_residual(x, o, w, g):
    m, d = x.shape
    tm = ROW_TILE
    return pl.pallas_call(
        _proj_residual_kernel,
        grid=(m // tm,),
        in_specs=[pl.BlockSpec((tm, d), lambda i: (i, 0)),
                  pl.BlockSpec((tm, o.shape[1]), lambda i: (i, 0)),
                  _full(w.shape), _full(g.shape)],
        out_specs=pl.BlockSpec((tm, d), lambda i: (i, 0)),
        out_shape=jax.ShapeDtypeStruct((m, d), F32),
        compiler_params=_params("parallel"),
        name="proj_residual",
    )(x, o, w, g)


def _gelu(x):
    return 0.5 * x * (1.0 + jnp.tanh(math.sqrt(2.0 / math.pi) * (x + 0.044715 * (x * x * x))))


def _ffn_kernel(seq_len, x_ref, xh_ref, gin_ref, gout_ref, wg_ref, wu_ref, cw_ref, wo_ref,
                out_ref, xn_ref, xnh_ref, acc_ref):
    tm = x_ref.shape[0]
    j = pl.program_id(1)

    @pl.when(j == 0)
    def _():
        xn_ref[...] = _bf(_rms(x_ref[...], gin_ref[...]))
        at_start = (pl.program_id(0) * tm) % seq_len == 0
        xnh_ref[...] = _bf(jnp.where(at_start, 0.0, _rms(xh_ref[...], gin_ref[...])))
        acc_ref[...] = jnp.zeros_like(acc_ref)

    xn = xn_ref[...]
    gate = _mm(xn, wg_ref[...])
    up = _mm(xn, wu_ref[...])
    gate_halo = _mm(xnh_ref[...], wg_ref[...])
    row = lax.broadcasted_iota(jnp.int32, gate.shape, 0)
    prev1 = jnp.where(row == 0, gate_halo[SUBLANES - 1:SUBLANES, :], pltpu.roll(gate, 1, 0))
    prev2 = jnp.where(row == 0, gate_halo[SUBLANES - 2:SUBLANES - 1, :],
                      jnp.where(row == 1, gate_halo[SUBLANES - 1:SUBLANES, :], pltpu.roll(gate, 2, 0)))
    gc = cw_ref[3:4, :] + prev2 * cw_ref[0:1, :] + prev1 * cw_ref[1:2, :] + gate * cw_ref[2:3, :]
    acc_ref[...] += _mm(_bf(_gelu(gc) * up), wo_ref[...])

    @pl.when(j == pl.num_programs(1) - 1)
    def _():
        out_ref[...] = x_ref[...] + _rms(acc_ref[...], gout_ref[...])


def _ffn(x, seq_len, g_in, g_out, w_in, conv, w_out):
    m, d = x.shape
    f = w_out.shape[0]
    tm, tf = FFN_ROW_TILE, FFN_COL_TILE
    halo = tm // SUBLANES
    nf = f // tf
    return pl.pallas_call(
        functools.partial(_ffn_kernel, seq_len),
        grid=(m // tm, nf),
        in_specs=[pl.BlockSpec((tm, d), lambda i, j: (i, 0)),
                  pl.BlockSpec((SUBLANES, d), lambda i, j: (jnp.maximum(i * halo - 1, 0), 0)),
                  _full(g_in.shape), _full(g_out.shape),
                  pl.BlockSpec((d, tf), lambda i, j: (0, j)),
                  pl.BlockSpec((d, tf), lambda i, j: (0, j + nf)),
                  pl.BlockSpec((SUBLANES, tf), lambda i, j: (0, j)),
                  pl.BlockSpec((tf, d), lambda i, j: (j, 0))],
        out_specs=pl.BlockSpec((tm, d), lambda i, j: (i, 0)),
        out_shape=jax.ShapeDtypeStruct((m, d), F32),
        scratch_shapes=[pltpu.VMEM((tm, d), BF16), pltpu.VMEM((SUBLANES, d), BF16),
                        pltpu.VMEM((tm, d), F32)],
        compiler_params=_params("parallel", "arbitrary"),
        name="conv_ffn",
    )(x, x, g_in, g_out, w_in, w_in, conv, w_out)


def _kv_prep_kernel(x_ref, g_ref, ga_ref, wd_ref, wra_ref, wrb_ref, cs_ref, wk_ref, wv_ref, place_ref,
                    k_ref, v_ref):
    xn = _bf(_rms(x_ref[...], g_ref[...]))
    ckv = _bf(_rms(_mm(xn, wd_ref[...]), ga_ref[...]))
    kr = _mm(xn, wra_ref[...]) * cs_ref[:, :LANES] + _mm(xn, wrb_ref[...]) * cs_ref[:, LANES:]
    k_ref[...] = _bf(_mm(ckv, wk_ref[...]) + _mm(_bf(kr), place_ref[...]))
    v_ref[...] = _bf(_mm(ckv, wv_ref[...]))


def _kv_prep(x, seq_len, g, ga, wd, wra, wrb, cs, wk, wv, place):
    m, d = x.shape
    tm = ROW_TILE
    nt = seq_len // tm
    return pl.pallas_call(
        _kv_prep_kernel,
        grid=(m // tm,),
        in_specs=[pl.BlockSpec((tm, d), lambda i: (i, 0)),
                  _full(g.shape), _full(ga.shape), _full(wd.shape), _full(wra.shape), _full(wrb.shape),
                  pl.BlockSpec((tm, 2 * LANES), lambda i: (i % nt, 0)),
                  _full(wk.shape), _full(wv.shape), _full(place.shape)],
        out_specs=[pl.BlockSpec((tm, wk.shape[1]), lambda i: (i, 0)),
                   pl.BlockSpec((tm, wv.shape[1]), lambda i: (i, 0))],
        out_shape=[jax.ShapeDtypeStruct((m, wk.shape[1]), BF16),
                   jax.ShapeDtypeStruct((m, wv.shape[1]), BF16)],
        compiler_params=_params("parallel"),
        name="mla_kv_prep",
    )(x, g, ga, wd, wra, wrb, cs, wk, wv, place)


def _q_prep_kernel(scale, x_ref, g_ref, gq_ref, wd_ref, wa_ref, wb_ref, cs_ref, q_ref):
    xn = _bf(_rms(x_ref[...], g_ref[...]))
    cq = _bf(_rms(_mm(xn, wd_ref[...]), gq_ref[...]))
    qa = _mm(cq, wa_ref[...])
    qb = _mm(cq, wb_ref[...])
    cos_t = cs_ref[:, :LANES]
    sin_t = cs_ref[:, LANES:]
    for h in range(qa.shape[1] // LANES):
        sl = slice(h * LANES, (h + 1) * LANES)
        q_ref[:, sl] = _bf((qa[:, sl] * cos_t + qb[:, sl] * sin_t) * scale)


def _q_prep(x, seq_len, scale, g, gq, wd, wa, wb, cs):
    m, d = x.shape
    tm = ROW_TILE
    nt = seq_len // tm
    return pl.pallas_call(
        functools.partial(_q_prep_kernel, scale),
        grid=(m // tm,),
        in_specs=[pl.BlockSpec((tm, d), lambda i: (i, 0)),
                  _full(g.shape), _full(gq.shape), _full(wd.shape), _full(wa.shape), _full(wb.shape),
                  pl.BlockSpec((tm, 2 * LANES), lambda i: (i % nt, 0))],
        out_specs=pl.BlockSpec((tm, wa.shape[1]), lambda i: (i, 0)),
        out_shape=jax.ShapeDtypeStruct((m, wa.shape[1]), BF16),
        compiler_params=_params("parallel"),
        name="mla_q_prep",
    )(x, g, gq, wd, wa, wb, cs)


def _attn_kernel(q_ref, k_ref, v_ref, o_ref):
    tq = q_ref.shape[0]
    tk = ATT_TILE
    qi = pl.program_id(2)
    qpos = lax.broadcasted_iota(jnp.int32, (tq, tk), 0) // MASK_CHUNK
    kpos = lax.broadcasted_iota(jnp.int32, (tq, tk), 1) // MASK_CHUNK
    visible = kpos <= qpos
    first_half = lax.broadcasted_iota(jnp.int32, (tq, PAIR_W), 1) < HEAD
    outs = []
    for h in range(2):
        hsl = slice(h * LANES, (h + 1) * LANES)
        q = q_ref[:, hsl]

        def step(s, vt, carry):
            m_i, l_i, acc = carry
            m_new = jnp.maximum(m_i, jnp.max(s, axis=-1, keepdims=True))
            alpha = jnp.exp(m_i - m_new)
            p = jnp.exp(s - m_new)
            l_new = alpha * l_i + jnp.sum(p, axis=-1, keepdims=True)
            return m_new, l_new, alpha * acc + _mm(_bf(p), vt)

        def body(j, carry, hsl=hsl, q=q):
            off = pl.multiple_of(j * tk, tk)
            s = _mm_nt(q, k_ref[pl.ds(off, tk), hsl])
            return step(s, v_ref[pl.ds(off, tk), :], carry)

        init = (jnp.full((tq, 1), NEG_INF, F32), jnp.zeros((tq, 1), F32), jnp.zeros((tq, PAIR_W), F32))
        carry = lax.fori_loop(0, qi, body, init)
        off = pl.multiple_of(qi * tk, tk)
        s = _mm_nt(q, k_ref[pl.ds(off, tk), hsl])
        s = jnp.where(visible, s, NEG_INF)
        _, l_i, acc = step(s, v_ref[pl.ds(off, tk), :], carry)
        outs.append(acc / l_i)
    o_ref[...] = _bf(jnp.where(first_half, outs[0], outs[1]))


def _attention(q, k, v, batch, seq_len):
    m = q.shape[0]
    n_pairs = v.shape[1] // PAIR_W
    tq = ATT_TILE
    nq = seq_len // tq
    return pl.pallas_call(
        _attn_kernel,
        grid=(batch, n_pairs, nq),
        in_specs=[pl.BlockSpec((tq, 2 * LANES), lambda b, p, i: (b * nq + i, p)),
                  pl.BlockSpec((seq_len, 2 * LANES), lambda b, p, i: (b, p)),
                  pl.BlockSpec((seq_len, PAIR_W), lambda b, p, i: (b, p))],
        out_specs=pl.BlockSpec((tq, PAIR_W), lambda b, p, i: (b * nq + i, p)),
        out_shape=jax.ShapeDtypeStruct((m, v.shape[1]), BF16),
        compiler_params=_params("parallel", "parallel", "arbitrary"),
        name="mla_attention",
    )(q, k, v)


def _pad_cols(w, n):
    return jnp.pad(w, ((0, 0), (0, n - w.shape[1])))


def _pad_rows(w, n):
    return jnp.pad(w, ((0, n - w.shape[0]), (0, 0)))


def _rows(vectors, d):
    rows = jnp.stack([v.reshape(d).astype(F32) for v in vectors])
    return _pad_rows(rows, -(-rows.shape[0] // SUBLANES) * SUBLANES)


def _rotate_half_cols(w):
    half = w.shape[1] // 2
    return jnp.concatenate([-w[:, half:], w[:, :half]], axis=1)


def _rope_table(seq_len):
    inv = 1.0 / (ROPE_THETA ** (jnp.arange(0, QK_ROPE, 2, dtype=F32) / QK_ROPE))
    ang = jnp.arange(seq_len, dtype=F32)[:, None] * inv[None, :]
    cos, sin = jnp.cos(ang), jnp.sin(ang)
    ones = jnp.ones((seq_len, QK_NOPE), F32)
    pad = jnp.zeros((seq_len, LANES - QK_NOPE - QK_ROPE), F32)
    cos_t = jnp.concatenate([ones, cos, cos, pad], axis=1)
    sin_t = jnp.concatenate([0 * ones, sin, sin, pad], axis=1)
    return jnp.concatenate([cos_t, sin_t], axis=1)


def kernel(x, norm_g, ffn_w_in, ffn_conv_w, ffn_conv_b, ffn_w_out, a_mu, a_w_rkv, a_w0, a_w1, a_w2, a_a0, a_a1, a_a2, a_g1, a_g2, a_k_k, a_k_a, a_r_k, a_lnx_w, a_lnx_b, a_w_o, kv_norm_g, kv_w_down, kv_a_norm_g, kv_w_up, q_w_down, q_norm_g, q_w_up, o_w):
    batch, seq_len, d = x.shape
    n_heads = d // HEAD
    n_a = a_mu.shape[0]
    depth = norm_g.shape[0]
    xf = x.reshape(batch * seq_len, d)

    head_of = jnp.arange(d) // HEAD
    e = _bf(head_of[:, None] == jnp.arange(LANES)[None, :])
    et = e.T
    pair_of = jnp.arange(PAIR_W) // HEAD
    bd = _bf(pair_of[:, None] == pair_of[None, :])
    rope_cs = _rope_table(seq_len)
    zeros_rope = jnp.zeros((1, LANES - QK_NOPE - QK_ROPE), F32)

    kq = vq = None
    for layer in range(depth):
        gl = norm_g[layer].astype(F32)
        if layer < n_a:
            i = layer
            vec = _rows([gl[0]] + [a_mu[i, n] for n in range(6)]
                        + [a_w0[i], a_a0[i], a_k_k[i], a_k_a[i]], d)
            lora = LANES
            gate_lora = 2 * LANES
            r, k, v, a, b, g, lw = _rwkv_prep(
                xf, seq_len, vec, _bf(a_w_rkv[i]),
                _bf(_pad_cols(a_w1[i], lora)), _bf(_pad_rows(a_w2[i], lora)),
                _bf(_pad_cols(a_a1[i], lora)), _bf(_pad_rows(a_a2[i], lora)),
                _bf(_pad_cols(a_g1[i], gate_lora)), _bf(_pad_rows(a_g2[i], gate_lora)), e, et)
            svec = _rows([a_lnx_w[i], a_lnx_b[i], a_r_k[i]], d)
            mix = _rwkv_scan(r, k, v, a, b, g, lw, svec, bd, batch, seq_len)
            w_out_proj = _bf(a_w_o[i])
        else:
            if layer == n_a:
                wd = kv_w_down[:, :KV_LORA]
                wr = kv_w_down[:, KV_LORA:]
                up = kv_w_up.reshape(KV_LORA, n_heads, 2 * HEAD)
                wk = _pad_cols(up[:, :, :QK_NOPE].reshape(KV_LORA * n_heads, QK_NOPE), LANES)
                wk = wk.reshape(KV_LORA, n_heads * LANES)
                wv = up[:, :, QK_NOPE:].reshape(KV_LORA, n_heads * HEAD)
                lane = jnp.arange(n_heads * LANES) % LANES
                place = (lane[None, :] == (jnp.arange(LANES) + QK_NOPE)[:, None]) & (jnp.arange(LANES) < QK_ROPE)[:, None]
                kq, vq = _kv_prep(
                    xf, seq_len, kv_norm_g.reshape(1, d), kv_a_norm_g.reshape(1, KV_LORA),
                    _bf(wd), _bf(_pad_cols(wr, LANES)), _bf(_pad_cols(_rotate_half_cols(wr), LANES)),
                    jnp.concatenate([rope_cs[:, QK_NOPE:LANES], jnp.zeros((seq_len, QK_NOPE), F32),
                                     rope_cs[:, LANES + QK_NOPE:], jnp.zeros((seq_len, QK_NOPE), F32)], axis=1),
                    _bf(wk), _bf(wv), _bf(place))
            j = layer - n_a
            qup = q_w_up[j].reshape(-1, n_heads, QK_NOPE + QK_ROPE)
            q_lora = qup.shape[0]
            wa = _pad_cols(qup.reshape(q_lora * n_heads, -1), LANES).reshape(q_lora, n_heads * LANES)
            rot = jnp.concatenate([jnp.zeros((q_lora, n_heads, QK_NOPE), F32),
                                   jnp.concatenate([-qup[:, :, QK_NOPE + QK_ROPE // 2:],
                                                    qup[:, :, QK_NOPE:QK_NOPE + QK_ROPE // 2]], axis=-1)], axis=-1)
            wb = _pad_cols(rot.reshape(q_lora * n_heads, -1), LANES).reshape(q_lora, n_heads * LANES)
            q = _q_prep(xf, seq_len, 1.0 / math.sqrt(QK_NOPE + QK_ROPE), gl[0:1], q_norm_g[j].reshape(1, -1),
                        _bf(q_w_down[j]), _bf(wa), _bf(wb), rope_cs)
            mix = _attention(q, kq, vq, batch, seq_len)
            w_out_proj = _bf(o_w[j])
        xf = _proj_residual(xf, mix, w_out_proj, gl[1:2])
        conv = _rows([ffn_conv_w[layer, 0], ffn_conv_w[layer, 1], ffn_conv_w[layer, 2], ffn_conv_b[layer]],
                     ffn_conv_b.shape[1])
        xf = _ffn(xf, seq_len, gl[2:3], gl[3:4], _bf(ffn_w_in[layer]), conv, _bf(ffn_w_out[layer]))
    return xf.reshape(batch, seq_len, d)
```

```python
import functools
import math

import jax
import jax.numpy as jnp
from jax import lax
from jax.experimental import pallas as pl
from jax.experimental.pallas import tpu as pltpu

F32 = jnp.float32
BF16 = jnp.bfloat16

HEAD = 64
LNX_EPS = 64e-5
NORM_EPS = 1e-6
QK_NOPE = 64
QK_ROPE = 32
KV_LORA = 256
ROPE_THETA = 10000.0
MASK_CHUNK = 64
NEG_INF = -1e30

LANES = 128
SUBLANES = 8
VMEM_LIMIT = 48 * 1024 * 1024

SCAN_CHUNK = 64
ROW_TILE = 512
FFN_ROW_TILE = 1024
FFN_COL_TILE = 256
ATT_TILE = 512
ATT_Q_SPLIT = 2


def _bf(x):
    return x.astype(BF16)


def _mm(a, b):
    return jnp.dot(a, b, preferred_element_type=F32)


def _mm_nt(a, b):
    return lax.dot_general(a, b, (((1,), (1,)), ((), ())), preferred_element_type=F32)


def _mm_tn(a, b):
    return lax.dot_general(a, b, (((0,), (0,)), ((), ())), preferred_element_type=F32)


def _rms(x, g):
    return x * lax.rsqrt(jnp.mean(x * x, axis=-1, keepdims=True) + NORM_EPS) * g


def _sigmoid(x):
    return 1.0 / (1.0 + jnp.exp(-x))


def _params(*sem):
    return pltpu.CompilerParams(dimension_semantics=sem, vmem_limit_bytes=VMEM_LIMIT)


def _full(shape):
    nd = len(shape)
    return pl.BlockSpec(shape, lambda *_: (0,) * nd)


def _rwkv_prep_kernel(seq_len, x_ref, xh_ref, vec_ref, wrkv_ref, w1_ref, w2_ref, a1_ref, a2_ref,
                      g1_ref, g2_ref, e_ref, et_ref,
                      r_ref, k_ref, v_ref, a_ref, b_ref, g_ref, lw_ref):
    tm = x_ref.shape[0]
    gn = vec_ref[0:1, :]
    h = _rms(x_ref[...], gn)
    h_halo = _rms(xh_ref[SUBLANES - 1:SUBLANES, :], gn)
    at_start = (pl.program_id(0) * tm) % seq_len == 0
    h_halo = jnp.where(at_start, 0.0, h_halo)
    row = lax.broadcasted_iota(jnp.int32, h.shape, 0)
    h_prev = jnp.where(row == 0, h_halo, pltpu.roll(h, 1, 0))
    xx = h_prev - h

    def mix(i):
        return _bf(h + xx * vec_ref[1 + i:2 + i, :])

    r = _mm(mix(0), wrkv_ref[0])
    k = _mm(mix(1), wrkv_ref[1])
    v = _mm(mix(2), wrkv_ref[2])
    zw = vec_ref[7:8, :] + _mm(_bf(jnp.tanh(_mm(mix(3), w1_ref[...]))), w2_ref[...])
    softplus = jnp.maximum(-zw, 0.0) + jnp.log(1.0 + jnp.exp(-jnp.abs(zw)))
    lw = -jnp.exp(-softplus - 0.5)
    a = _sigmoid(vec_ref[8:9, :] + _mm(_bf(_mm(mix(4), a1_ref[...])), a2_ref[...]))
    g = _mm(_bf(_sigmoid(_mm(mix(5), g1_ref[...]))), g2_ref[...])
    kk = k * vec_ref[9:10, :]
    ss = _mm(_bf(_mm(_bf(kk * kk), e_ref[...])), et_ref[...])
    kk = kk / jnp.maximum(jnp.sqrt(ss), 1e-12)
    k = k * (1.0 + (a - 1.0) * vec_ref[10:11, :])
    r_ref[...] = _bf(r)
    k_ref[...] = _bf(k)
    v_ref[...] = _bf(v)
    a_ref[...] = _bf(-kk)
    b_ref[...] = _bf(kk * a)
    g_ref[...] = _bf(g)
    lw_ref[...] = lw


def _rwkv_prep(x, seq_len, vec, wrkv, w1, w2, a1, a2, g1, g2, e, et):
    m, d = x.shape
    tm = ROW_TILE
    halo = tm // SUBLANES
    row_spec = pl.BlockSpec((tm, d), lambda i: (i, 0))
    out_bf = jax.ShapeDtypeStruct((m, d), BF16)
    return pl.pallas_call(
        functools.partial(_rwkv_prep_kernel, seq_len),
        grid=(m // tm,),
        in_specs=[row_spec,
                  pl.BlockSpec((SUBLANES, d), lambda i: (jnp.maximum(i * halo - 1, 0), 0)),
                  _full(vec.shape), _full(wrkv.shape), _full(w1.shape), _full(w2.shape),
                  _full(a1.shape), _full(a2.shape), _full(g1.shape), _full(g2.shape),
                  _full(e.shape), _full(et.shape)],
        out_specs=[row_spec] * 7,
        out_shape=[out_bf] * 6 + [jax.ShapeDtypeStruct((m, d), F32)],
        compiler_params=_params("parallel"),
        name="rwkv_prep",
    )(x, x, vec, wrkv, w1, w2, a1, a2, g1, g2, e, et)


GROUP = 4
GROUP_W = GROUP * HEAD
PAIR_W = 2 * HEAD


def _unit_lower_inverse(nms, row, col):
    diff = row ^ col
    eye = jnp.where(row == col, 1.0, 0.0)
    n8 = [jnp.where((diff >> 3) == 0, nm, 0.0) for nm in nms]
    n8b = [_bf(n) for n in n8]
    n8_2 = [_mm(n, n) for n in n8b]
    n8_2b = [_bf(n) for n in n8_2]
    n8_4 = [_bf(_mm(n, n)) for n in n8_2b]
    n8_3 = [_mm(n, n2) for n, n2 in zip(n8b, n8_2b)]
    xs = [eye + n + n2 + n3 for n, n2, n3 in zip(n8, n8_2, n8_3)]
    xs = [x + _mm(_bf(x), n4) for x, n4 in zip(xs, n8_4)]
    for shift in (3, 4, 5):
        offs = [_bf(jnp.where((diff >> shift) == 1, nm, 0.0)) for nm in nms]
        xbs = [_bf(x) for x in xs]
        ts = [_bf(_mm(xb, off)) for xb, off in zip(xbs, offs)]
        xs = [x + _mm(t, xb) for x, t, xb in zip(xs, ts, xbs)]
    return xs


def _rwkv_scan_kernel(r_ref, k_ref, v_ref, a_ref, b_ref, g_ref, lw_ref, vec_ref, bd_ref,
                      o_ref, s_ref):
    chunk, d = lw_ref.shape

    @pl.when(pl.program_id(1) == 0)
    def _():
        s_ref[...] = jnp.zeros_like(s_ref)

    lw = lw_ref[...]
    trow = lax.broadcasted_iota(jnp.int32, (chunk, chunk), 0)
    tcol = lax.broadcasted_iota(jnp.int32, (chunk, chunk), 1)
    tri = _bf(jnp.where(trow >= tcol, 1.0, 0.0))
    lw_hi = _bf(lw)
    lw_lo = _bf(lw - lw_hi.astype(F32))
    c = _mm(tri, lw_hi) + _mm(tri, lw_lo)
    c_last = c[chunk - 1:chunk, :]
    r = r_ref[...].astype(F32)
    k = k_ref[...].astype(F32)
    a = a_ref[...].astype(F32)
    b = b_ref[...].astype(F32)
    e_inv = jnp.exp(-c)
    e_rem = jnp.exp(c_last - c)
    at = a * jnp.exp(c - lw)
    rt = r * jnp.exp(c)
    bt = b * e_inv
    kt = k * e_inv
    bh = _bf(b * e_rem)
    kh = _bf(k * e_rem)
    g_last = jnp.exp(c_last)
    atb = _bf(at)
    rtb = _bf(rt)

    lane_g = lax.broadcasted_iota(jnp.int32, (chunk, GROUP_W), 1) >> 6
    row = lax.broadcasted_iota(jnp.int32, (GROUP_W, GROUP_W), 0)
    col = lax.broadcasted_iota(jnp.int32, (GROUP_W, GROUP_W), 1)
    strict = (row & (HEAD - 1)) > (col & (HEAD - 1))
    incl = (row & (HEAD - 1)) >= (col & (HEAD - 1))
    first_half = lax.broadcasted_iota(jnp.int32, (chunk, PAIR_W), 1) < HEAD
    prow = lax.broadcasted_iota(jnp.int32, (PAIR_W, PAIR_W), 0)
    pcol = lax.broadcasted_iota(jnp.int32, (PAIR_W, PAIR_W), 1)
    same_head = (prow >> 6) == (pcol >> 6)
    bd = bd_ref[...]

    def stack(x, sl):
        xs = x[:, sl]
        return jnp.concatenate([_bf(jnp.where(lane_g == h, xs, 0.0)) for h in range(GROUP)], axis=0)

    n_groups = d // GROUP_W
    n_pairs = d // PAIR_W
    gsl = [slice(q * GROUP_W, (q + 1) * GROUP_W) for q in range(n_groups)]
    psl = [slice(i * PAIR_W, (i + 1) * PAIR_W) for i in range(n_pairs)]

    lhs = [jnp.concatenate([stack(at, s), stack(rt, s)], axis=0) for s in gsl]
    rhs = [jnp.concatenate([stack(bt, s), stack(kt, s)], axis=0) for s in gsl]
    p = [_mm_nt(l, r_) for l, r_ in zip(lhs, rhs)]
    n_ab = [jnp.where(strict, x[:GROUP_W, :GROUP_W], 0.0) for x in p]
    a_ak = [_bf(jnp.where(strict, x[:GROUP_W, GROUP_W:], 0.0)) for x in p]
    aa = [jnp.concatenate([_bf(jnp.where(incl, x[GROUP_W:, :GROUP_W], 0.0)),
                           _bf(jnp.where(incl, x[GROUP_W:, GROUP_W:], 0.0))], axis=1) for x in p]
    tinv = [_bf(x) for x in _unit_lower_inverse(n_ab, row, col)]
    tt = [jnp.concatenate([t, _bf(_mm(t, ak))], axis=1) for t, ak in zip(tinv, a_ak)]

    s0 = [s_ref[i] for i in range(n_pairs)]
    vp = [v_ref[:, s] for s in psl]
    qq = [_mm_nt(jnp.concatenate([atb[:, sl], rtb[:, sl]], axis=0), _bf(s)) for sl, s in zip(psl, s0)]
    qa = [_bf(x[:chunk]) for x in qq]
    qr = [x[chunk:] for x in qq]

    def stack_pairs(xs, q):
        return [xs[2 * q], xs[2 * q], xs[2 * q + 1], xs[2 * q + 1]]

    u_st = [_mm(tt[q], jnp.concatenate(stack_pairs(qa, q) + stack_pairs(vp, q), axis=0))
            for q in range(n_groups)]
    y_st = [jnp.concatenate(stack_pairs(qr, q), axis=0)
            + _mm(aa[q], jnp.concatenate([_bf(u_st[q])] + stack_pairs(vp, q), axis=0))
            for q in range(n_groups)]

    def unstack(x_st, i):
        j = i % 2
        lo, mid, hi = 2 * j * chunk, (2 * j + 1) * chunk, (2 * j + 2) * chunk
        return jnp.where(first_half, x_st[i // 2][lo:mid], x_st[i // 2][mid:hi])

    u = [unstack(u_st, i) for i in range(n_pairs)]
    y = [unstack(y_st, i) for i in range(n_pairs)]
    ds = [_mm_tn(jnp.concatenate([_bf(u[i]), vp[i]], axis=0),
                 jnp.concatenate([bh[:, psl[i]], kh[:, psl[i]]], axis=0)) for i in range(n_pairs)]
    for i in range(n_pairs):
        s_ref[i] = s0[i] * g_last[:, psl[i]] + jnp.where(same_head, ds[i], 0.0)

    mean = [_mm(_bf(y[i]), bd) * (1.0 / HEAD) for i in range(n_pairs)]
    dev = [y[i] - mean[i] for i in range(n_pairs)]
    var = [_mm(_bf(x * x), bd) * (1.0 / HEAD) for x in dev]
    bonus = [_mm(_bf(r[:, sl] * k[:, sl] * vec_ref[2:3, sl]), bd) for sl in psl]
    for i, sl in enumerate(psl):
        yn = dev[i] * lax.rsqrt(var[i] + LNX_EPS) * vec_ref[0:1, sl] + vec_ref[1:2, sl]
        o_ref[:, sl] = _bf((yn + bonus[i] * vp[i].astype(F32)) * g_ref[:, sl].astype(F32))


def _rwkv_scan(r, k, v, a, b, g, lw, vec, bd, batch, seq_len):
    m, d = lw.shape
    nc = seq_len // SCAN_CHUNK
    blk = pl.BlockSpec((SCAN_CHUNK, d), lambda bi, ci: (bi * nc + ci, 0))
    return pl.pallas_call(
        _rwkv_scan_kernel,
        grid=(batch, nc),
        in_specs=[blk] * 7 + [_full(vec.shape), _full(bd.shape)],
        out_specs=blk,
        out_shape=jax.ShapeDtypeStruct((m, d), BF16),
        scratch_shapes=[pltpu.VMEM((d // PAIR_W, PAIR_W, PAIR_W), F32)],
        compiler_params=_params("parallel", "arbitrary"),
        name="rwkv_scan",
    )(r, k, v, a, b, g, lw, vec, bd)


def _proj_residual_kernel(x_ref, o_ref, w_ref, g_ref, out_ref):
    out_ref[...] = x_ref[...] + _rms(_mm(o_ref[...], w_ref[...]), g_ref[...])


def _proj_residual(x, o, w, g):
    m, d = x.shape
    tm = ROW_TILE
    return pl.pallas_call(
        _proj_residual_kernel,
        grid=(m // tm,),
        in_specs=[pl.BlockSpec((tm, d), lambda i: (i, 0)),
                  pl.BlockSpec((tm, o.shape[1]), lambda i: (i, 0)),
                  _full(w.shape), _full(g.shape)],
        out_specs=pl.BlockSpec((tm, d), lambda i: (i, 0)),
        out_shape=jax.ShapeDtypeStruct((m, d), F32),
        compiler_params=_params("parallel"),
        name="proj_residual",
    )(x, o, w, g)


def _gelu(x):
    return 0.5 * x * (1.0 + jnp.tanh(math.sqrt(2.0 / math.pi) * (x + 0.044715 * (x * x * x))))


def _ffn_kernel(seq_len, x_ref, xh_ref, gin_ref, gout_ref, wg_ref, wu_ref, cw_ref, wo_ref,
                out_ref, xn_ref, xnh_ref, acc_ref):
    tm = x_ref.shape[0]
    j = pl.program_id(1)

    @pl.when(j == 0)
    def _():
        xn_ref[...] = _bf(_rms(x_ref[...], gin_ref[...]))
        at_start = (pl.program_id(0) * tm) % seq_len == 0
        xnh_ref[...] = _bf(jnp.where(at_start, 0.0, _rms(xh_ref[...], gin_ref[...])))
        acc_ref[...] = jnp.zeros_like(acc_ref)

    xn = xn_ref[...]
    gate = _mm(xn, wg_ref[...])
    up = _mm(xn, wu_ref[...])
    gate_halo = _mm(xnh_ref[...], wg_ref[...])
    row = lax.broadcasted_iota(jnp.int32, gate.shape, 0)
    prev1 = jnp.where(row == 0, gate_halo[SUBLANES - 1:SUBLANES, :], pltpu.roll(gate, 1, 0))
    prev2 = jnp.where(row == 0, gate_halo[SUBLANES - 2:SUBLANES - 1, :],
                      jnp.where(row == 1, gate_halo[SUBLANES - 1:SUBLANES, :], pltpu.roll(gate, 2, 0)))
    gc = cw_ref[3:4, :] + prev2 * cw_ref[0:1, :] + prev1 * cw_ref[1:2, :] + gate * cw_ref[2:3, :]
    acc_ref[...] += _mm(_bf(_gelu(gc) * up), wo_ref[...])

    @pl.when(j == pl.num_programs(1) - 1)
    def _():
        out_ref[...] = x_ref[...] + _rms(acc_ref[...], gout_ref[...])


def _ffn(x, seq_len, g_in, g_out, w_in, conv, w_out):
    m, d = x.shape
    f = w_out.shape[0]
    tm, tf = FFN_ROW_TILE, FFN_COL_TILE
    halo = tm // SUBLANES
    nf = f // tf
    return pl.pallas_call(
        functools.partial(_ffn_kernel, seq_len),
        grid=(m // tm, nf),
        in_specs=[pl.BlockSpec((tm, d), lambda i, j: (i, 0)),
                  pl.BlockSpec((SUBLANES, d), lambda i, j: (jnp.maximum(i * halo - 1, 0), 0)),
                  _full(g_in.shape), _full(g_out.shape),
                  pl.BlockSpec((d, tf), lambda i, j: (0, j)),
                  pl.BlockSpec((d, tf), lambda i, j: (0, j + nf)),
                  pl.BlockSpec((SUBLANES, tf), lambda i, j: (0, j)),
                  pl.BlockSpec((tf, d), lambda i, j: (j, 0))],
        out_specs=pl.BlockSpec((tm, d), lambda i, j: (i, 0)),
        out_shape=jax.ShapeDtypeStruct((m, d), F32),
        scratch_shapes=[pltpu.VMEM((tm, d), BF16), pltpu.VMEM((SUBLANES, d), BF16),
                        pltpu.VMEM((tm, d), F32)],
        compiler_params=_params("parallel", "arbitrary"),
        name="conv_ffn",
    )(x, x, g_in, g_out, w_in, w_in, conv, w_out)


def _kv_prep_kernel(x_ref, g_ref, ga_ref, wd_ref, wra_ref, wrb_ref, cs_ref, wk_ref, wv_ref, place_ref,
                    k_ref, v_ref):
    xn = _bf(_rms(x_ref[...], g_ref[...]))
    ckv = _bf(_rms(_mm(xn, wd_ref[...]), ga_ref[...]))
    kr = _mm(xn, wra_ref[...]) * cs_ref[:, :LANES] + _mm(xn, wrb_ref[...]) * cs_ref[:, LANES:]
    k_ref[...] = _bf(_mm(ckv, wk_ref[...]) + _mm(_bf(kr), place_ref[...]))
    v_ref[...] = _bf(_mm(ckv, wv_ref[...]))


def _kv_prep(x, seq_len, g, ga, wd, wra, wrb, cs, wk, wv, place):
    m, d = x.shape
    tm = ROW_TILE
    nt = seq_len // tm
    return pl.pallas_call(
        _kv_prep_kernel,
        grid=(m // tm,),
        in_specs=[pl.BlockSpec((tm, d), lambda i: (i, 0)),
                  _full(g.shape), _full(ga.shape), _full(wd.shape), _full(wra.shape), _full(wrb.shape),
                  pl.BlockSpec((tm, 2 * LANES), lambda i: (i % nt, 0)),
                  _full(wk.shape), _full(wv.shape), _full(place.shape)],
        out_specs=[pl.BlockSpec((tm, wk.shape[1]), lambda i: (i, 0)),
                   pl.BlockSpec((tm, wv.shape[1]), lambda i: (i, 0))],
        out_shape=[jax.ShapeDtypeStruct((m, wk.shape[1]), BF16),
                   jax.ShapeDtypeStruct((m, wv.shape[1]), BF16)],
        compiler_params=_params("parallel"),
        name="mla_kv_prep",
    )(x, g, ga, wd, wra, wrb, cs, wk, wv, place)


def _q_prep_kernel(scale, x_ref, g_ref, gq_ref, wd_ref, wa_ref, wb_ref, cs_ref, q_ref):
    xn = _bf(_rms(x_ref[...], g_ref[...]))
    cq = _bf(_rms(_mm(xn, wd_ref[...]), gq_ref[...]))
    qa = _mm(cq, wa_ref[...])
    qb = _mm(cq, wb_ref[...])
    cos_t = cs_ref[:, :LANES]
    sin_t = cs_ref[:, LANES:]
    for h in range(qa.shape[1] // LANES):
        sl = slice(h * LANES, (h + 1) * LANES)
        q_ref[:, sl] = _bf((qa[:, sl] * cos_t + qb[:, sl] * sin_t) * scale)


def _q_prep(x, seq_len, scale, g, gq, wd, wa, wb, cs):
    m, d = x.shape
    tm = ROW_TILE
    nt = seq_len // tm
    return pl.pallas_call(
        functools.partial(_q_prep_kernel, scale),
        grid=(m // tm,),
        in_specs=[pl.BlockSpec((tm, d), lambda i: (i, 0)),
                  _full(g.shape), _full(gq.shape), _full(wd.shape), _full(wa.shape), _full(wb.shape),
                  pl.BlockSpec((tm, 2 * LANES), lambda i: (i % nt, 0))],
        out_specs=pl.BlockSpec((tm, wa.shape[1]), lambda i: (i, 0)),
        out_shape=jax.ShapeDtypeStruct((m, wa.shape[1]), BF16),
        compiler_params=_params("parallel"),
        name="mla_q_prep",
    )(x, g, gq, wd, wa, wb, cs)


def _attn_kernel(q_ref, k_ref, vt_ref, o_ref):
    tq = q_ref.shape[0]
    tk = vt_ref.shape[-1]
    tc = tq // ATT_Q_SPLIT
    qi = pl.program_id(2)
    streams = [(h, c) for h in range(2) for c in range(ATT_Q_SPLIT)]
    hsl = [slice(h * LANES, (h + 1) * LANES) for h in range(2)]
    qs = [q_ref[c * tc:(c + 1) * tc, hsl[h]] for h, c in streams]

    def step(s, vt, m_i, acc):
        m_new = jnp.maximum(m_i, jnp.max(s, axis=0, keepdims=True))
        alpha = jnp.exp2(m_i - m_new)
        p = jnp.exp2(s - m_new)
        return m_new, alpha * acc + _mm(vt, _bf(p))

    def body(j, carry):
        off = pl.multiple_of(j * tk, tk)
        kt = [k_ref[pl.ds(off, tk), sl] for sl in hsl]
        ss = [_mm_nt(kt[h], q) for (h, _), q in zip(streams, qs)]
        return tuple(step(s, vt_ref[h, j], *mc) for (h, _), s, mc in zip(streams, ss, carry))

    init = tuple((jnp.full((1, tc), NEG_INF, F32), jnp.zeros((LANES, tc), F32)) for _ in streams)
    carry = lax.fori_loop(0, qi, body, init)

    off = pl.multiple_of(qi * tk, tk)
    outs = []
    for (h, c), q, mc in zip(streams, qs, carry):
        nk = (c + 1) * tc
        kpos = lax.broadcasted_iota(jnp.int32, (nk, tc), 0) // MASK_CHUNK
        qpos = lax.broadcasted_iota(jnp.int32, (nk, tc), 1) // MASK_CHUNK + c * (tc // MASK_CHUNK)
        s = jnp.where(kpos <= qpos, _mm_nt(k_ref[pl.ds(off, nk), hsl[h]], q), NEG_INF)
        _, acc = step(s, vt_ref[h, qi][:, :nk], *mc)
        outs.append(acc[:HEAD] / acc[HEAD:HEAD + 1])
    o_t = jnp.concatenate([jnp.concatenate(outs[h * ATT_Q_SPLIT:(h + 1) * ATT_Q_SPLIT], axis=1)
                           for h in range(2)], axis=0)
    o_ref[...] = _bf(o_t.T)


def _attention(q, k, vt, batch, seq_len):
    m = q.shape[0]
    n_heads, nkt, _, tk = vt.shape[1:]
    n_pairs = n_heads // 2
    tq = ATT_TILE
    nq = seq_len // tq
    return pl.pallas_call(
        _attn_kernel,
        grid=(batch, n_pairs, nq),
        in_specs=[pl.BlockSpec((tq, 2 * LANES), lambda b, p, i: (b * nq + i, p)),
                  pl.BlockSpec((seq_len, 2 * LANES), lambda b, p, i: (b, p)),
                  pl.BlockSpec((None, 2, nkt, LANES, tk), lambda b, p, i: (b, p, 0, 0, 0))],
        out_specs=pl.BlockSpec((tq, PAIR_W), lambda b, p, i: (b * nq + i, p)),
        out_shape=jax.ShapeDtypeStruct((m, n_pairs * PAIR_W), BF16),
        compiler_params=_params("parallel", "parallel", "arbitrary"),
        name="mla_attention",
    )(q, k, vt)


def _value_tiles(v, batch, seq_len):
    tk = ATT_TILE
    n_heads = v.shape[1] // HEAD
    vt = v.reshape(batch, seq_len, n_heads, HEAD).transpose(0, 2, 3, 1)
    ones = jnp.ones((batch, n_heads, 1, seq_len), v.dtype)
    zeros = jnp.zeros((batch, n_heads, LANES - HEAD - 1, seq_len), v.dtype)
    vt = jnp.concatenate([vt, ones, zeros], axis=2)
    return vt.reshape(batch, n_heads, LANES, seq_len // tk, tk).transpose(0, 1, 3, 2, 4)


def _pad_cols(w, n):
    return jnp.pad(w, ((0, 0), (0, n - w.shape[1])))


def _pad_rows(w, n):
    return jnp.pad(w, ((0, n - w.shape[0]), (0, 0)))


def _rows(vectors, d):
    rows = jnp.stack([v.reshape(d).astype(F32) for v in vectors])
    return _pad_rows(rows, -(-rows.shape[0] // SUBLANES) * SUBLANES)


def _rotate_half_cols(w):
    half = w.shape[1] // 2
    return jnp.concatenate([-w[:, half:], w[:, :half]], axis=1)


def _rope_table(seq_len):
    inv = 1.0 / (ROPE_THETA ** (jnp.arange(0, QK_ROPE, 2, dtype=F32) / QK_ROPE))
    ang = jnp.arange(seq_len, dtype=F32)[:, None] * inv[None, :]
    cos, sin = jnp.cos(ang), jnp.sin(ang)
    ones = jnp.ones((seq_len, QK_NOPE), F32)
    pad = jnp.zeros((seq_len, LANES - QK_NOPE - QK_ROPE), F32)
    cos_t = jnp.concatenate([ones, cos, cos, pad], axis=1)
    sin_t = jnp.concatenate([0 * ones, sin, sin, pad], axis=1)
    return jnp.concatenate([cos_t, sin_t], axis=1)


def kernel(x, norm_g, ffn_w_in, ffn_conv_w, ffn_conv_b, ffn_w_out, a_mu, a_w_rkv, a_w0, a_w1, a_w2, a_a0, a_a1, a_a2, a_g1, a_g2, a_k_k, a_k_a, a_r_k, a_lnx_w, a_lnx_b, a_w_o, kv_norm_g, kv_w_down, kv_a_norm_g, kv_w_up, q_w_down, q_norm_g, q_w_up, o_w):
    batch, seq_len, d = x.shape
    n_heads = d // HEAD
    n_a = a_mu.shape[0]
    depth = norm_g.shape[0]
    xf = x.reshape(batch * seq_len, d)

    head_of = jnp.arange(d) // HEAD
    e = _bf(head_of[:, None] == jnp.arange(LANES)[None, :])
    et = e.T
    pair_of = jnp.arange(PAIR_W) // HEAD
    bd = _bf(pair_of[:, None] == pair_of[None, :])
    rope_cs = _rope_table(seq_len)
    zeros_rope = jnp.zeros((1, LANES - QK_NOPE - QK_ROPE), F32)

    kq = vtq = None
    for layer in range(depth):
        gl = norm_g[layer].astype(F32)
        if layer < n_a:
            i = layer
            vec = _rows([gl[0]] + [a_mu[i, n] for n in range(6)]
                        + [a_w0[i], a_a0[i], a_k_k[i], a_k_a[i]], d)
            lora = LANES
            gate_lora = 2 * LANES
            r, k, v, a, b, g, lw = _rwkv_prep(
                xf, seq_len, vec, _bf(a_w_rkv[i]),
                _bf(_pad_cols(a_w1[i], lora)), _bf(_pad_rows(a_w2[i], lora)),
                _bf(_pad_cols(a_a1[i], lora)), _bf(_pad_rows(a_a2[i], lora)),
                _bf(_pad_cols(a_g1[i], gate_lora)), _bf(_pad_rows(a_g2[i], gate_lora)), e, et)
            svec = _rows([a_lnx_w[i], a_lnx_b[i], a_r_k[i]], d)
            mix = _rwkv_scan(r, k, v, a, b, g, lw, svec, bd, batch, seq_len)
            w_out_proj = _bf(a_w_o[i])
        else:
            if layer == n_a:
                wd = kv_w_down[:, :KV_LORA]
                wr = kv_w_down[:, KV_LORA:]
                up = kv_w_up.reshape(KV_LORA, n_heads, 2 * HEAD)
                wk = _pad_cols(up[:, :, :QK_NOPE].reshape(KV_LORA * n_heads, QK_NOPE), LANES)
                wk = wk.reshape(KV_LORA, n_heads * LANES)
                wv = up[:, :, QK_NOPE:].reshape(KV_LORA, n_heads * HEAD)
                lane = jnp.arange(n_heads * LANES) % LANES
                place = (lane[None, :] == (jnp.arange(LANES) + QK_NOPE)[:, None]) & (jnp.arange(LANES) < QK_ROPE)[:, None]
                kq, vq = _kv_prep(
                    xf, seq_len, kv_norm_g.reshape(1, d), kv_a_norm_g.reshape(1, KV_LORA),
                    _bf(wd), _bf(_pad_cols(wr, LANES)), _bf(_pad_cols(_rotate_half_cols(wr), LANES)),
                    jnp.concatenate([rope_cs[:, QK_NOPE:LANES], jnp.zeros((seq_len, QK_NOPE), F32),
                                     rope_cs[:, LANES + QK_NOPE:], jnp.zeros((seq_len, QK_NOPE), F32)], axis=1),
                    _bf(wk), _bf(wv), _bf(place))
                vtq = _value_tiles(vq, batch, seq_len)
            j = layer - n_a
            qup = q_w_up[j].reshape(-1, n_heads, QK_NOPE + QK_ROPE)
            q_lora = qup.shape[0]
            wa = _pad_cols(qup.reshape(q_lora * n_heads, -1), LANES).reshape(q_lora, n_heads * LANES)
            rot = jnp.concatenate([jnp.zeros((q_lora, n_heads, QK_NOPE), F32),
                                   jnp.concatenate([-qup[:, :, QK_NOPE + QK_ROPE // 2:],
                                                    qup[:, :, QK_NOPE:QK_NOPE + QK_ROPE // 2]], axis=-1)], axis=-1)
            wb = _pad_cols(rot.reshape(q_lora * n_heads, -1), LANES).reshape(q_lora, n_heads * LANES)
            q = _q_prep(xf, seq_len, math.log2(math.e) / math.sqrt(QK_NOPE + QK_ROPE), gl[0:1], q_norm_g[j].reshape(1, -1),
                        _bf(q_w_down[j]), _bf(wa), _bf(wb), rope_cs)
            mix = _attention(q, kq, vtq, batch, seq_len)
            w_out_proj = _bf(o_w[j])
        xf = _proj_residual(xf, mix, w_out_proj, gl[1:2])
        conv = _rows([ffn_conv_w[layer, 0], ffn_conv_w[layer, 1], ffn_conv_w[layer, 2], ffn_conv_b[layer]],
                     ffn_conv_b.shape[1])
        xf = _ffn(xf, seq_len, gl[2:3], gl[3:4], _bf(ffn_w_in[layer]), conv, _bf(ffn_w_out[layer]))
    return xf.reshape(batch, seq_len, d)
```

```python
import functools
import math

import jax
import jax.numpy as jnp
from jax import lax
from jax.experimental import pallas as pl
from jax.experimental.pallas import tpu as pltpu

F32 = jnp.float32
BF16 = jnp.bfloat16

HEAD = 64
LNX_EPS = 64e-5
NORM_EPS = 1e-6
QK_NOPE = 64
QK_ROPE = 32
KV_LORA = 256
ROPE_THETA = 10000.0
MASK_CHUNK = 64
NEG_INF = -1e30

LANES = 128
SUBLANES = 8
VMEM_LIMIT = 48 * 1024 * 1024

SCAN_CHUNK = 64
ROW_TILE = 512
FFN_ROW_TILE = 512
FFN_COL_BLOCK = 256
FFN_LOOKAHEAD = 2
ATT_TILE = 512
ATT_Q_SPLIT = 2


def _bf(x):
    return x.astype(BF16)


def _mm(a, b):
    return jnp.dot(a, b, preferred_element_type=F32)


def _mm_nt(a, b):
    return lax.dot_general(a, b, (((1,), (1,)), ((), ())), preferred_element_type=F32)


def _mm_tn(a, b):
    return lax.dot_general(a, b, (((0,), (0,)), ((), ())), preferred_element_type=F32)


def _rms(x, g):
    return x * lax.rsqrt(jnp.mean(x * x, axis=-1, keepdims=True) + NORM_EPS) * g


def _sigmoid(x):
    return 1.0 / (1.0 + jnp.exp(-x))


def _params(*sem):
    return pltpu.CompilerParams(dimension_semantics=sem, vmem_limit_bytes=VMEM_LIMIT)


def _full(shape):
    nd = len(shape)
    return pl.BlockSpec(shape, lambda *_: (0,) * nd)


def _resident(shape):
    nd = len(shape)
    return pl.BlockSpec(shape, lambda *_: (0,) * nd, pipeline_mode=pl.Buffered(1))


def _rwkv_prep_kernel(seq_len, x_ref, xh_ref, vec_ref, wrkv_ref, w1_ref, w2_ref, a1_ref, a2_ref,
                      g1_ref, g2_ref, e_ref, et_ref,
                      r_ref, k_ref, v_ref, a_ref, b_ref, g_ref, lw_ref):
    tm = x_ref.shape[0]
    gn = vec_ref[0:1, :]
    h = _rms(x_ref[...], gn)
    h_halo = _rms(xh_ref[SUBLANES - 1:SUBLANES, :], gn)
    at_start = (pl.program_id(0) * tm) % seq_len == 0
    h_halo = jnp.where(at_start, 0.0, h_halo)
    row = lax.broadcasted_iota(jnp.int32, h.shape, 0)
    h_prev = jnp.where(row == 0, h_halo, pltpu.roll(h, 1, 0))
    xx = h_prev - h

    def mix(i):
        return _bf(h + xx * vec_ref[1 + i:2 + i, :])

    r = _mm(mix(0), wrkv_ref[0])
    k = _mm(mix(1), wrkv_ref[1])
    v = _mm(mix(2), wrkv_ref[2])
    zw = vec_ref[7:8, :] + _mm(_bf(jnp.tanh(_mm(mix(3), w1_ref[...]))), w2_ref[...])
    softplus = jnp.maximum(-zw, 0.0) + jnp.log(1.0 + jnp.exp(-jnp.abs(zw)))
    lw = -jnp.exp(-softplus - 0.5)
    a = _sigmoid(vec_ref[8:9, :] + _mm(_bf(_mm(mix(4), a1_ref[...])), a2_ref[...]))
    g = _mm(_bf(_sigmoid(_mm(mix(5), g1_ref[...]))), g2_ref[...])
    kk = k * vec_ref[9:10, :]
    ss = _mm(_bf(_mm(_bf(kk * kk), e_ref[...])), et_ref[...])
    kk = kk / jnp.maximum(jnp.sqrt(ss), 1e-12)
    k = k * (1.0 + (a - 1.0) * vec_ref[10:11, :])
    r_ref[...] = _bf(r)
    k_ref[...] = _bf(k)
    v_ref[...] = _bf(v)
    a_ref[...] = _bf(-kk)
    b_ref[...] = _bf(kk * a)
    g_ref[...] = _bf(g)
    lw_ref[...] = lw


def _rwkv_prep(x, seq_len, vec, wrkv, w1, w2, a1, a2, g1, g2, e, et):
    m, d = x.shape
    tm = ROW_TILE
    halo = tm // SUBLANES
    row_spec = pl.BlockSpec((tm, d), lambda i: (i, 0))
    out_bf = jax.ShapeDtypeStruct((m, d), BF16)
    return pl.pallas_call(
        functools.partial(_rwkv_prep_kernel, seq_len),
        grid=(m // tm,),
        in_specs=[row_spec,
                  pl.BlockSpec((SUBLANES, d), lambda i: (jnp.maximum(i * halo - 1, 0), 0)),
                  _full(vec.shape), _full(wrkv.shape), _full(w1.shape), _full(w2.shape),
                  _full(a1.shape), _full(a2.shape), _full(g1.shape), _full(g2.shape),
                  _full(e.shape), _full(et.shape)],
        out_specs=[row_spec] * 7,
        out_shape=[out_bf] * 6 + [jax.ShapeDtypeStruct((m, d), F32)],
        compiler_params=_params("parallel"),
        name="rwkv_prep",
    )(x, x, vec, wrkv, w1, w2, a1, a2, g1, g2, e, et)


GROUP = 4
GROUP_W = GROUP * HEAD
PAIR_W = 2 * HEAD


def _unit_lower_inverse(nms, row, col):
    diff = row ^ col
    eye = jnp.where(row == col, 1.0, 0.0)
    n8 = [jnp.where((diff >> 3) == 0, nm, 0.0) for nm in nms]
    n8b = [_bf(n) for n in n8]
    n8_2 = [_mm(n, n) for n in n8b]
    n8_2b = [_bf(n) for n in n8_2]
    n8_4 = [_bf(_mm(n, n)) for n in n8_2b]
    n8_3 = [_mm(n, n2) for n, n2 in zip(n8b, n8_2b)]
    xs = [eye + n + n2 + n3 for n, n2, n3 in zip(n8, n8_2, n8_3)]
    xs = [x + _mm(_bf(x), n4) for x, n4 in zip(xs, n8_4)]
    for shift in (3, 4, 5):
        offs = [_bf(jnp.where((diff >> shift) == 1, nm, 0.0)) for nm in nms]
        xbs = [_bf(x) for x in xs]
        ts = [_bf(_mm(xb, off)) for xb, off in zip(xbs, offs)]
        xs = [x + _mm(t, xb) for x, t, xb in zip(xs, ts, xbs)]
    return xs


def _rwkv_scan_kernel(r_ref, k_ref, v_ref, a_ref, b_ref, g_ref, lw_ref, vec_ref, bd_ref,
                      o_ref, s_ref):
    chunk, d = lw_ref.shape

    @pl.when(pl.program_id(1) == 0)
    def _():
        s_ref[...] = jnp.zeros_like(s_ref)

    lw = lw_ref[...]
    trow = lax.broadcasted_iota(jnp.int32, (chunk, chunk), 0)
    tcol = lax.broadcasted_iota(jnp.int32, (chunk, chunk), 1)
    tri = _bf(jnp.where(trow >= tcol, 1.0, 0.0))
    lw_hi = _bf(lw)
    lw_lo = _bf(lw - lw_hi.astype(F32))
    c = _mm(tri, lw_hi) + _mm(tri, lw_lo)
    c_last = c[chunk - 1:chunk, :]
    r = r_ref[...].astype(F32)
    k = k_ref[...].astype(F32)
    a = a_ref[...].astype(F32)
    b = b_ref[...].astype(F32)
    e_inv = jnp.exp(-c)
    e_rem = jnp.exp(c_last - c)
    at = a * jnp.exp(c - lw)
    rt = r * jnp.exp(c)
    bt = b * e_inv
    kt = k * e_inv
    bh = _bf(b * e_rem)
    kh = _bf(k * e_rem)
    g_last = jnp.exp(c_last)
    atb = _bf(at)
    rtb = _bf(rt)

    lane_g = lax.broadcasted_iota(jnp.int32, (chunk, GROUP_W), 1) >> 6
    row = lax.broadcasted_iota(jnp.int32, (GROUP_W, GROUP_W), 0)
    col = lax.broadcasted_iota(jnp.int32, (GROUP_W, GROUP_W), 1)
    strict = (row & (HEAD - 1)) > (col & (HEAD - 1))
    incl = (row & (HEAD - 1)) >= (col & (HEAD - 1))
    first_half = lax.broadcasted_iota(jnp.int32, (chunk, PAIR_W), 1) < HEAD
    prow = lax.broadcasted_iota(jnp.int32, (PAIR_W, PAIR_W), 0)
    pcol = lax.broadcasted_iota(jnp.int32, (PAIR_W, PAIR_W), 1)
    same_head = (prow >> 6) == (pcol >> 6)
    bd = bd_ref[...]

    def stack(x, sl):
        xs = x[:, sl]
        return jnp.concatenate([_bf(jnp.where(lane_g == h, xs, 0.0)) for h in range(GROUP)], axis=0)

    n_groups = d // GROUP_W
    n_pairs = d // PAIR_W
    gsl = [slice(q * GROUP_W, (q + 1) * GROUP_W) for q in range(n_groups)]
    psl = [slice(i * PAIR_W, (i + 1) * PAIR_W) for i in range(n_pairs)]

    lhs = [jnp.concatenate([stack(at, s), stack(rt, s)], axis=0) for s in gsl]
    rhs = [jnp.concatenate([stack(bt, s), stack(kt, s)], axis=0) for s in gsl]
    p = [_mm_nt(l, r_) for l, r_ in zip(lhs, rhs)]
    n_ab = [jnp.where(strict, x[:GROUP_W, :GROUP_W], 0.0) for x in p]
    a_ak = [_bf(jnp.where(strict, x[:GROUP_W, GROUP_W:], 0.0)) for x in p]
    aa = [jnp.concatenate([_bf(jnp.where(incl, x[GROUP_W:, :GROUP_W], 0.0)),
                           _bf(jnp.where(incl, x[GROUP_W:, GROUP_W:], 0.0))], axis=1) for x in p]
    tinv = [_bf(x) for x in _unit_lower_inverse(n_ab, row, col)]
    tt = [jnp.concatenate([t, _bf(_mm(t, ak))], axis=1) for t, ak in zip(tinv, a_ak)]

    s0 = [s_ref[i] for i in range(n_pairs)]
    vp = [v_ref[:, s] for s in psl]
    qq = [_mm_nt(jnp.concatenate([atb[:, sl], rtb[:, sl]], axis=0), _bf(s)) for sl, s in zip(psl, s0)]
    qa = [_bf(x[:chunk]) for x in qq]
    qr = [x[chunk:] for x in qq]

    def stack_pairs(xs, q):
        return [xs[2 * q], xs[2 * q], xs[2 * q + 1], xs[2 * q + 1]]

    u_st = [_mm(tt[q], jnp.concatenate(stack_pairs(qa, q) + stack_pairs(vp, q), axis=0))
            for q in range(n_groups)]
    y_st = [jnp.concatenate(stack_pairs(qr, q), axis=0)
            + _mm(aa[q], jnp.concatenate([_bf(u_st[q])] + stack_pairs(vp, q), axis=0))
            for q in range(n_groups)]

    def unstack(x_st, i):
        j = i % 2
        lo, mid, hi = 2 * j * chunk, (2 * j + 1) * chunk, (2 * j + 2) * chunk
        return jnp.where(first_half, x_st[i // 2][lo:mid], x_st[i // 2][mid:hi])

    u = [unstack(u_st, i) for i in range(n_pairs)]
    y = [unstack(y_st, i) for i in range(n_pairs)]
    ds = [_mm_tn(jnp.concatenate([_bf(u[i]), vp[i]], axis=0),
                 jnp.concatenate([bh[:, psl[i]], kh[:, psl[i]]], axis=0)) for i in range(n_pairs)]
    for i in range(n_pairs):
        s_ref[i] = s0[i] * g_last[:, psl[i]] + jnp.where(same_head, ds[i], 0.0)

    mean = [_mm(_bf(y[i]), bd) * (1.0 / HEAD) for i in range(n_pairs)]
    dev = [y[i] - mean[i] for i in range(n_pairs)]
    var = [_mm(_bf(x * x), bd) * (1.0 / HEAD) for x in dev]
    bonus = [_mm(_bf(r[:, sl] * k[:, sl] * vec_ref[2:3, sl]), bd) for sl in psl]
    for i, sl in enumerate(psl):
        yn = dev[i] * lax.rsqrt(var[i] + LNX_EPS) * vec_ref[0:1, sl] + vec_ref[1:2, sl]
        o_ref[:, sl] = _bf((yn + bonus[i] * vp[i].astype(F32)) * g_ref[:, sl].astype(F32))


def _rwkv_scan(r, k, v, a, b, g, lw, vec, bd, batch, seq_len):
    m, d = lw.shape
    nc = seq_len // SCAN_CHUNK
    blk = pl.BlockSpec((SCAN_CHUNK, d), lambda bi, ci: (bi * nc + ci, 0))
    return pl.pallas_call(
        _rwkv_scan_kernel,
        grid=(batch, nc),
        in_specs=[blk] * 7 + [_full(vec.shape), _full(bd.shape)],
        out_specs=blk,
        out_shape=jax.ShapeDtypeStruct((m, d), BF16),
        scratch_shapes=[pltpu.VMEM((d // PAIR_W, PAIR_W, PAIR_W), F32)],
        compiler_params=_params("parallel", "arbitrary"),
        name="rwkv_scan",
    )(r, k, v, a, b, g, lw, vec, bd)


def _gelu_times(x, y):
    k1 = -2.0 * math.sqrt(2.0 / math.pi) * math.log2(math.e)
    e = jnp.exp2(x * (k1 + (k1 * 0.044715) * (x * x)))
    return (x * y) / (1.0 + e)


def _mix_ffn_kernel(tiles_per_seq, x_ref, o_ref, wp_ref, vec_ref, wg_ref, wu_ref, cw_ref, wo_ref,
                    out_ref, tail_ref):
    tm = x_ref.shape[0]
    fc = FFN_COL_BLOCK
    x1 = x_ref[...] + _rms(_mm(o_ref[...], wp_ref[...]), vec_ref[0:1, :])
    xn = _bf(_rms(x1, vec_ref[1:2, :]))

    @pl.when(pl.program_id(0) % tiles_per_seq == 0)
    def _():
        tail_ref[...] = jnp.zeros_like(tail_ref)

    row = lax.broadcasted_iota(jnp.int32, (tm, fc), 0)
    acc = jnp.zeros(x1.shape, F32)
    n_blocks = wg_ref.shape[1] // fc

    def gate_up(c):
        cs = slice(c * fc, (c + 1) * fc)
        return _mm(xn, wg_ref[:, cs]), _mm(xn, wu_ref[:, cs])

    ahead = [gate_up(c) for c in range(min(FFN_LOOKAHEAD, n_blocks))]
    for c in range(n_blocks):
        cs = slice(c * fc, (c + 1) * fc)
        gate, up = ahead.pop(0)
        if c + FFN_LOOKAHEAD < n_blocks:
            ahead.append(gate_up(c + FFN_LOOKAHEAD))
        t1 = tail_ref[SUBLANES - 1:SUBLANES, cs]
        t2 = tail_ref[SUBLANES - 2:SUBLANES - 1, cs]
        prev1 = jnp.where(row == 0, t1, pltpu.roll(gate, 1, 0))
        prev2 = jnp.where(row == 0, t2, jnp.where(row == 1, t1, pltpu.roll(gate, 2, 0)))
        tail_ref[:, cs] = gate[tm - SUBLANES:, :]
        gc = cw_ref[3:4, cs] + prev2 * cw_ref[0:1, cs] + prev1 * cw_ref[1:2, cs] + gate * cw_ref[2:3, cs]
        acc = acc + _mm(_bf(_gelu_times(gc, up)), wo_ref[cs, :])
    out_ref[...] = x1 + _rms(acc, vec_ref[2:3, :])


def _mix_ffn(x, o, seq_len, w_proj, vec, w_gate, w_up, conv, w_out):
    m, d = x.shape
    tm = FFN_ROW_TILE
    f = w_out.shape[0]
    return pl.pallas_call(
        functools.partial(_mix_ffn_kernel, seq_len // tm),
        grid=(m // tm,),
        in_specs=[pl.BlockSpec((tm, d), lambda i: (i, 0)),
                  pl.BlockSpec((tm, o.shape[1]), lambda i: (i, 0)),
                  _resident(w_proj.shape), _resident(vec.shape), _resident(w_gate.shape),
                  _resident(w_up.shape), _resident(conv.shape), _resident(w_out.shape)],
        out_specs=pl.BlockSpec((tm, d), lambda i: (i, 0)),
        out_shape=jax.ShapeDtypeStruct((m, d), F32),
        scratch_shapes=[pltpu.VMEM((SUBLANES, f), F32)],
        compiler_params=_params("arbitrary"),
        name="mix_ffn",
    )(x, o, w_proj, vec, w_gate, w_up, conv, w_out)


def _kv_prep_kernel(x_ref, g_ref, ga_ref, wd_ref, wra_ref, wrb_ref, cs_ref, wk_ref, wvt_ref, place_ref,
                    k_ref, vt_ref):
    xn = _bf(_rms(x_ref[...], g_ref[...]))
    ckv = _bf(_rms(_mm(xn, wd_ref[...]), ga_ref[...]))
    kr = _mm(xn, wra_ref[...]) * cs_ref[:, :LANES] + _mm(xn, wrb_ref[...]) * cs_ref[:, LANES:]
    k_ref[...] = _bf(_mm(ckv, wk_ref[...]) + _mm(_bf(kr), place_ref[...]))
    vt = _mm_nt(wvt_ref[...], ckv)
    ones_row = (lax.broadcasted_iota(jnp.int32, vt.shape, 0) & (LANES - 1)) == HEAD
    vt_ref[...] = _bf(jnp.where(ones_row, 1.0, vt))


def _kv_prep(x, batch, seq_len, g, ga, wd, wra, wrb, cs, wk, wvt, place):
    m, d = x.shape
    tm = ATT_TILE
    nt = seq_len // tm
    return pl.pallas_call(
        _kv_prep_kernel,
        grid=(m // tm,),
        in_specs=[pl.BlockSpec((tm, d), lambda i: (i, 0)),
                  _full(g.shape), _full(ga.shape), _full(wd.shape), _full(wra.shape), _full(wrb.shape),
                  pl.BlockSpec((tm, 2 * LANES), lambda i: (i % nt, 0)),
                  _full(wk.shape), _full(wvt.shape), _full(place.shape)],
        out_specs=[pl.BlockSpec((tm, wk.shape[1]), lambda i: (i, 0)),
                   pl.BlockSpec((None, None, wvt.shape[0], tm), lambda i: (i // nt, i % nt, 0, 0))],
        out_shape=[jax.ShapeDtypeStruct((m, wk.shape[1]), BF16),
                   jax.ShapeDtypeStruct((batch, nt, wvt.shape[0], tm), BF16)],
        compiler_params=_params("parallel"),
        name="mla_kv_prep",
    )(x, g, ga, wd, wra, wrb, cs, wk, wvt, place)


def _q_prep_kernel(scale, x_ref, g_ref, gq_ref, wd_ref, wa_ref, wb_ref, cs_ref, q_ref):
    xn = _bf(_rms(x_ref[...], g_ref[...]))
    cq = _bf(_rms(_mm(xn, wd_ref[...]), gq_ref[...]))
    qa = _mm(cq, wa_ref[...])
    qb = _mm(cq, wb_ref[...])
    cos_t = cs_ref[:, :LANES]
    sin_t = cs_ref[:, LANES:]
    for h in range(qa.shape[1] // LANES):
        sl = slice(h * LANES, (h + 1) * LANES)
        q_ref[:, sl] = _bf((qa[:, sl] * cos_t + qb[:, sl] * sin_t) * scale)


def _q_prep(x, seq_len, scale, g, gq, wd, wa, wb, cs):
    m, d = x.shape
    tm = ROW_TILE
    nt = seq_len // tm
    return pl.pallas_call(
        functools.partial(_q_prep_kernel, scale),
        grid=(m // tm,),
        in_specs=[pl.BlockSpec((tm, d), lambda i: (i, 0)),
                  _full(g.shape), _full(gq.shape), _full(wd.shape), _full(wa.shape), _full(wb.shape),
                  pl.BlockSpec((tm, 2 * LANES), lambda i: (i % nt, 0))],
        out_specs=pl.BlockSpec((tm, wa.shape[1]), lambda i: (i, 0)),
        out_shape=jax.ShapeDtypeStruct((m, wa.shape[1]), BF16),
        compiler_params=_params("parallel"),
        name="mla_q_prep",
    )(x, g, gq, wd, wa, wb, cs)


def _attn_kernel(q_ref, k_ref, vt_ref, o_ref):
    tq = q_ref.shape[0]
    tk = vt_ref.shape[-1]
    tc = tq // ATT_Q_SPLIT
    qi = pl.program_id(2)
    streams = [(h, c) for h in range(2) for c in range(ATT_Q_SPLIT)]
    hsl = [slice(h * LANES, (h + 1) * LANES) for h in range(2)]
    qs = [q_ref[c * tc:(c + 1) * tc, hsl[h]] for h, c in streams]

    def step(s, vt, m_i, acc):
        m_new = jnp.maximum(m_i, jnp.max(s, axis=0, keepdims=True))
        alpha = jnp.exp2(m_i - m_new)
        p = jnp.exp2(s - m_new)
        return m_new, alpha * acc + _mm(vt, _bf(p))

    def body(j, carry):
        off = pl.multiple_of(j * tk, tk)
        kt = [k_ref[pl.ds(off, tk), sl] for sl in hsl]
        ss = [_mm_nt(kt[h], q) for (h, _), q in zip(streams, qs)]
        return tuple(step(s, vt_ref[j, hsl[h], :], *mc) for (h, _), s, mc in zip(streams, ss, carry))

    init = tuple((jnp.full((1, tc), NEG_INF, F32), jnp.zeros((LANES, tc), F32)) for _ in streams)
    carry = lax.fori_loop(0, qi, body, init)

    off = pl.multiple_of(qi * tk, tk)
    outs = []
    for (h, c), q, mc in zip(streams, qs, carry):
        nk = (c + 1) * tc
        kpos = lax.broadcasted_iota(jnp.int32, (nk, tc), 0) // MASK_CHUNK
        qpos = lax.broadcasted_iota(jnp.int32, (nk, tc), 1) // MASK_CHUNK + c * (tc // MASK_CHUNK)
        s = jnp.where(kpos <= qpos, _mm_nt(k_ref[pl.ds(off, nk), hsl[h]], q), NEG_INF)
        _, acc = step(s, vt_ref[qi, hsl[h], :nk], *mc)
        outs.append(acc[:HEAD] / acc[HEAD:HEAD + 1])
    o_t = jnp.concatenate([jnp.concatenate(outs[h * ATT_Q_SPLIT:(h + 1) * ATT_Q_SPLIT], axis=1)
                           for h in range(2)], axis=0)
    o_ref[...] = _bf(o_t.T)


def _attention(q, k, vt, batch, seq_len):
    m = q.shape[0]
    nkt, rows, tk = vt.shape[1:]
    n_pairs = rows // (2 * LANES)
    tq = ATT_TILE
    nq = seq_len // tq
    return pl.pallas_call(
        _attn_kernel,
        grid=(batch, n_pairs, nq),
        in_specs=[pl.BlockSpec((tq, 2 * LANES), lambda b, p, i: (b * nq + i, p)),
                  pl.BlockSpec((seq_len, 2 * LANES), lambda b, p, i: (b, p)),
                  pl.BlockSpec((None, nkt, 2 * LANES, tk), lambda b, p, i: (b, 0, p, 0))],
        out_specs=pl.BlockSpec((tq, PAIR_W), lambda b, p, i: (b * nq + i, p)),
        out_shape=jax.ShapeDtypeStruct((m, n_pairs * PAIR_W), BF16),
        compiler_params=_params("parallel", "parallel", "arbitrary"),
        name="mla_attention",
    )(q, k, vt)


def _pad_cols(w, n):
    return jnp.pad(w, ((0, 0), (0, n - w.shape[1])))


def _pad_rows(w, n):
    return jnp.pad(w, ((0, n - w.shape[0]), (0, 0)))


def _rows(vectors, d):
    rows = jnp.stack([v.reshape(d).astype(F32) for v in vectors])
    return _pad_rows(rows, -(-rows.shape[0] // SUBLANES) * SUBLANES)


def _rotate_half_cols(w):
    half = w.shape[1] // 2
    return jnp.concatenate([-w[:, half:], w[:, :half]], axis=1)


def _rope_table(seq_len):
    inv = 1.0 / (ROPE_THETA ** (jnp.arange(0, QK_ROPE, 2, dtype=F32) / QK_ROPE))
    ang = jnp.arange(seq_len, dtype=F32)[:, None] * inv[None, :]
    cos, sin = jnp.cos(ang), jnp.sin(ang)
    ones = jnp.ones((seq_len, QK_NOPE), F32)
    pad = jnp.zeros((seq_len, LANES - QK_NOPE - QK_ROPE), F32)
    cos_t = jnp.concatenate([ones, cos, cos, pad], axis=1)
    sin_t = jnp.concatenate([0 * ones, sin, sin, pad], axis=1)
    return jnp.concatenate([cos_t, sin_t], axis=1)


def kernel(x, norm_g, ffn_w_in, ffn_conv_w, ffn_conv_b, ffn_w_out, a_mu, a_w_rkv, a_w0, a_w1, a_w2, a_a0, a_a1, a_a2, a_g1, a_g2, a_k_k, a_k_a, a_r_k, a_lnx_w, a_lnx_b, a_w_o, kv_norm_g, kv_w_down, kv_a_norm_g, kv_w_up, q_w_down, q_norm_g, q_w_up, o_w):
    batch, seq_len, d = x.shape
    n_heads = d // HEAD
    n_a = a_mu.shape[0]
    depth = norm_g.shape[0]
    xf = x.reshape(batch * seq_len, d)

    head_of = jnp.arange(d) // HEAD
    e = _bf(head_of[:, None] == jnp.arange(LANES)[None, :])
    et = e.T
    pair_of = jnp.arange(PAIR_W) // HEAD
    bd = _bf(pair_of[:, None] == pair_of[None, :])
    rope_cs = _rope_table(seq_len)
    zeros_rope = jnp.zeros((1, LANES - QK_NOPE - QK_ROPE), F32)

    kq = vtq = None
    for layer in range(depth):
        gl = norm_g[layer].astype(F32)
        if layer < n_a:
            i = layer
            vec = _rows([gl[0]] + [a_mu[i, n] for n in range(6)]
                        + [a_w0[i], a_a0[i], a_k_k[i], a_k_a[i]], d)
            lora = LANES
            gate_lora = 2 * LANES
            r, k, v, a, b, g, lw = _rwkv_prep(
                xf, seq_len, vec, _bf(a_w_rkv[i]),
                _bf(_pad_cols(a_w1[i], lora)), _bf(_pad_rows(a_w2[i], lora)),
                _bf(_pad_cols(a_a1[i], lora)), _bf(_pad_rows(a_a2[i], lora)),
                _bf(_pad_cols(a_g1[i], gate_lora)), _bf(_pad_rows(a_g2[i], gate_lora)), e, et)
            svec = _rows([a_lnx_w[i], a_lnx_b[i], a_r_k[i]], d)
            mix = _rwkv_scan(r, k, v, a, b, g, lw, svec, bd, batch, seq_len)
            w_out_proj = _bf(a_w_o[i])
        else:
            if layer == n_a:
                wd = kv_w_down[:, :KV_LORA]
                wr = kv_w_down[:, KV_LORA:]
                up = kv_w_up.reshape(KV_LORA, n_heads, 2 * HEAD)
                wk = _pad_cols(up[:, :, :QK_NOPE].reshape(KV_LORA * n_heads, QK_NOPE), LANES)
                wk = wk.reshape(KV_LORA, n_heads * LANES)
                wvt = _pad_cols(up[:, :, QK_NOPE:].reshape(KV_LORA * n_heads, HEAD), LANES)
                wvt = wvt.reshape(KV_LORA, n_heads * LANES).T
                lane = jnp.arange(n_heads * LANES) % LANES
                place = (lane[None, :] == (jnp.arange(LANES) + QK_NOPE)[:, None]) & (jnp.arange(LANES) < QK_ROPE)[:, None]
                kq, vtq = _kv_prep(
                    xf, batch, seq_len, kv_norm_g.reshape(1, d), kv_a_norm_g.reshape(1, KV_LORA),
                    _bf(wd), _bf(_pad_cols(wr, LANES)), _bf(_pad_cols(_rotate_half_cols(wr), LANES)),
                    jnp.concatenate([rope_cs[:, QK_NOPE:LANES], jnp.zeros((seq_len, QK_NOPE), F32),
                                     rope_cs[:, LANES + QK_NOPE:], jnp.zeros((seq_len, QK_NOPE), F32)], axis=1),
                    _bf(wk), _bf(wvt), _bf(place))
            j = layer - n_a
            qup = q_w_up[j].reshape(-1, n_heads, QK_NOPE + QK_ROPE)
            q_lora = qup.shape[0]
            wa = _pad_cols(qup.reshape(q_lora * n_heads, -1), LANES).reshape(q_lora, n_heads * LANES)
            rot = jnp.concatenate([jnp.zeros((q_lora, n_heads, QK_NOPE), F32),
                                   jnp.concatenate([-qup[:, :, QK_NOPE + QK_ROPE // 2:],
                                                    qup[:, :, QK_NOPE:QK_NOPE + QK_ROPE // 2]], axis=-1)], axis=-1)
            wb = _pad_cols(rot.reshape(q_lora * n_heads, -1), LANES).reshape(q_lora, n_heads * LANES)
            q = _q_prep(xf, seq_len, math.log2(math.e) / math.sqrt(QK_NOPE + QK_ROPE), gl[0:1], q_norm_g[j].reshape(1, -1),
                        _bf(q_w_down[j]), _bf(wa), _bf(wb), rope_cs)
            mix = _attention(q, kq, vtq, batch, seq_len)
            w_out_proj = _bf(o_w[j])
        d_ff = ffn_conv_b.shape[1]
        conv = _rows([ffn_conv_w[layer, 0], ffn_conv_w[layer, 1], ffn_conv_w[layer, 2], ffn_conv_b[layer]], d_ff)
        w_in = _bf(ffn_w_in[layer])
        xf = _mix_ffn(xf, mix, seq_len, w_out_proj, _rows([gl[1], gl[2], gl[3]], d),
                      w_in[:, :d_ff], w_in[:, d_ff:], conv, _bf(ffn_w_out[layer]))
    return xf.reshape(batch, seq_len, d)
```

```python
import functools
import math

import jax
import jax.numpy as jnp
from jax import lax
from jax.experimental import pallas as pl
from jax.experimental.pallas import tpu as pltpu

F32 = jnp.float32
BF16 = jnp.bfloat16

HEAD = 64
LNX_EPS = 64e-5
NORM_EPS = 1e-6
QK_NOPE = 64
QK_ROPE = 32
KV_LORA = 256
ROPE_THETA = 10000.0
MASK_CHUNK = 64
NEG_INF = -1e30

LANES = 128
SUBLANES = 8
VMEM_LIMIT = 48 * 1024 * 1024

SCAN_CHUNK = 64
ROW_TILE = 512
FFN_ROW_TILE = 512
FFN_COL_BLOCK = 256
FFN_LOOKAHEAD = 2
ATT_TILE = 1024
ATT_Q_SPLIT = 4


def _bf(x):
    return x.astype(BF16)


def _mm(a, b):
    return jnp.dot(a, b, preferred_element_type=F32)


def _mm_nt(a, b):
    return lax.dot_general(a, b, (((1,), (1,)), ((), ())), preferred_element_type=F32)


def _mm_tn(a, b):
    return lax.dot_general(a, b, (((0,), (0,)), ((), ())), preferred_element_type=F32)


def _rms(x, g):
    return x * lax.rsqrt(jnp.mean(x * x, axis=-1, keepdims=True) + NORM_EPS) * g


def _sigmoid(x):
    return 1.0 / (1.0 + jnp.exp(-x))


def _params(*sem):
    return pltpu.CompilerParams(dimension_semantics=sem, vmem_limit_bytes=VMEM_LIMIT)


def _full(shape):
    nd = len(shape)
    return pl.BlockSpec(shape, lambda *_: (0,) * nd)


def _resident(shape):
    nd = len(shape)
    return pl.BlockSpec(shape, lambda *_: (0,) * nd, pipeline_mode=pl.Buffered(1))


def _rwkv_prep_kernel(seq_len, x_ref, xh_ref, vec_ref, wrkv_ref, w1_ref, w2_ref, a1_ref, a2_ref,
                      g1_ref, g2_ref, e_ref, et_ref,
                      r_ref, k_ref, v_ref, a_ref, b_ref, g_ref, lw_ref):
    tm = x_ref.shape[0]
    gn = vec_ref[0:1, :]
    h = _rms(x_ref[...], gn)
    h_halo = _rms(xh_ref[SUBLANES - 1:SUBLANES, :], gn)
    at_start = (pl.program_id(0) * tm) % seq_len == 0
    h_halo = jnp.where(at_start, 0.0, h_halo)
    row = lax.broadcasted_iota(jnp.int32, h.shape, 0)
    h_prev = jnp.where(row == 0, h_halo, pltpu.roll(h, 1, 0))
    xx = h_prev - h

    def mix(i):
        return _bf(h + xx * vec_ref[1 + i:2 + i, :])

    r = _mm(mix(0), wrkv_ref[0])
    k = _mm(mix(1), wrkv_ref[1])
    v = _mm(mix(2), wrkv_ref[2])
    zw = vec_ref[7:8, :] + _mm(_bf(jnp.tanh(_mm(mix(3), w1_ref[...]))), w2_ref[...])
    softplus = jnp.maximum(-zw, 0.0) + jnp.log(1.0 + jnp.exp(-jnp.abs(zw)))
    lw = -jnp.exp(-softplus - 0.5)
    a = _sigmoid(vec_ref[8:9, :] + _mm(_bf(_mm(mix(4), a1_ref[...])), a2_ref[...]))
    g = _mm(_bf(_sigmoid(_mm(mix(5), g1_ref[...]))), g2_ref[...])
    kk = k * vec_ref[9:10, :]
    ss = _mm(_bf(_mm(_bf(kk * kk), e_ref[...])), et_ref[...])
    kk = kk / jnp.maximum(jnp.sqrt(ss), 1e-12)
    k = k * (1.0 + (a - 1.0) * vec_ref[10:11, :])
    r_ref[...] = _bf(r)
    k_ref[...] = _bf(k)
    v_ref[...] = _bf(v)
    a_ref[...] = _bf(-kk)
    b_ref[...] = _bf(kk * a)
    g_ref[...] = _bf(g)
    lw_ref[...] = lw


def _rwkv_prep(x, seq_len, vec, wrkv, w1, w2, a1, a2, g1, g2, e, et):
    m, d = x.shape
    tm = ROW_TILE
    halo = tm // SUBLANES
    row_spec = pl.BlockSpec((tm, d), lambda i: (i, 0))
    out_bf = jax.ShapeDtypeStruct((m, d), BF16)
    return pl.pallas_call(
        functools.partial(_rwkv_prep_kernel, seq_len),
        grid=(m // tm,),
        in_specs=[row_spec,
                  pl.BlockSpec((SUBLANES, d), lambda i: (jnp.maximum(i * halo - 1, 0), 0)),
                  _full(vec.shape), _full(wrkv.shape), _full(w1.shape), _full(w2.shape),
                  _full(a1.shape), _full(a2.shape), _full(g1.shape), _full(g2.shape),
                  _full(e.shape), _full(et.shape)],
        out_specs=[row_spec] * 7,
        out_shape=[out_bf] * 6 + [jax.ShapeDtypeStruct((m, d), F32)],
        compiler_params=_params("parallel"),
        name="rwkv_prep",
    )(x, x, vec, wrkv, w1, w2, a1, a2, g1, g2, e, et)


GROUP = 4
GROUP_W = GROUP * HEAD
PAIR_W = 2 * HEAD


def _unit_lower_inverse(nms, row, col):
    diff = row ^ col
    eye = jnp.where(row == col, 1.0, 0.0)
    n8 = [jnp.where((diff >> 3) == 0, nm, 0.0) for nm in nms]
    n8b = [_bf(n) for n in n8]
    n8_2 = [_mm(n, n) for n in n8b]
    n8_2b = [_bf(n) for n in n8_2]
    n8_4 = [_bf(_mm(n, n)) for n in n8_2b]
    n8_3 = [_mm(n, n2) for n, n2 in zip(n8b, n8_2b)]
    xs = [eye + n + n2 + n3 for n, n2, n3 in zip(n8, n8_2, n8_3)]
    xs = [x + _mm(_bf(x), n4) for x, n4 in zip(xs, n8_4)]
    for shift in (3, 4, 5):
        offs = [_bf(jnp.where((diff >> shift) == 1, nm, 0.0)) for nm in nms]
        xbs = [_bf(x) for x in xs]
        ts = [_bf(_mm(xb, off)) for xb, off in zip(xbs, offs)]
        xs = [x + _mm(t, xb) for x, t, xb in zip(xs, ts, xbs)]
    return xs


def _rwkv_scan_kernel(r_ref, k_ref, v_ref, a_ref, b_ref, g_ref, lw_ref, vec_ref, bd_ref,
                      o_ref, s_ref):
    chunk, d = lw_ref.shape

    @pl.when(pl.program_id(1) == 0)
    def _():
        s_ref[...] = jnp.zeros_like(s_ref)

    lw = lw_ref[...]
    trow = lax.broadcasted_iota(jnp.int32, (chunk, chunk), 0)
    tcol = lax.broadcasted_iota(jnp.int32, (chunk, chunk), 1)
    tri = _bf(jnp.where(trow >= tcol, 1.0, 0.0))
    lw_hi = _bf(lw)
    lw_lo = _bf(lw - lw_hi.astype(F32))
    c = _mm(tri, lw_hi) + _mm(tri, lw_lo)
    c_last = c[chunk - 1:chunk, :]
    r = r_ref[...].astype(F32)
    k = k_ref[...].astype(F32)
    a = a_ref[...].astype(F32)
    b = b_ref[...].astype(F32)
    e_inv = jnp.exp(-c)
    e_rem = jnp.exp(c_last - c)
    at = a * jnp.exp(c - lw)
    rt = r * jnp.exp(c)
    bt = b * e_inv
    kt = k * e_inv
    bh = _bf(b * e_rem)
    kh = _bf(k * e_rem)
    g_last = jnp.exp(c_last)
    atb = _bf(at)
    rtb = _bf(rt)

    lane_g = lax.broadcasted_iota(jnp.int32, (chunk, GROUP_W), 1) >> 6
    row = lax.broadcasted_iota(jnp.int32, (GROUP_W, GROUP_W), 0)
    col = lax.broadcasted_iota(jnp.int32, (GROUP_W, GROUP_W), 1)
    strict = (row & (HEAD - 1)) > (col & (HEAD - 1))
    incl = (row & (HEAD - 1)) >= (col & (HEAD - 1))
    first_half = lax.broadcasted_iota(jnp.int32, (chunk, PAIR_W), 1) < HEAD
    prow = lax.broadcasted_iota(jnp.int32, (PAIR_W, PAIR_W), 0)
    pcol = lax.broadcasted_iota(jnp.int32, (PAIR_W, PAIR_W), 1)
    same_head = (prow >> 6) == (pcol >> 6)
    bd = bd_ref[...]

    def stack(x, sl):
        xs = x[:, sl]
        return jnp.concatenate([_bf(jnp.where(lane_g == h, xs, 0.0)) for h in range(GROUP)], axis=0)

    n_groups = d // GROUP_W
    n_pairs = d // PAIR_W
    gsl = [slice(q * GROUP_W, (q + 1) * GROUP_W) for q in range(n_groups)]
    psl = [slice(i * PAIR_W, (i + 1) * PAIR_W) for i in range(n_pairs)]

    lhs = [jnp.concatenate([stack(at, s), stack(rt, s)], axis=0) for s in gsl]
    rhs = [jnp.concatenate([stack(bt, s), stack(kt, s)], axis=0) for s in gsl]
    p = [_mm_nt(l, r_) for l, r_ in zip(lhs, rhs)]
    n_ab = [jnp.where(strict, x[:GROUP_W, :GROUP_W], 0.0) for x in p]
    a_ak = [_bf(jnp.where(strict, x[:GROUP_W, GROUP_W:], 0.0)) for x in p]
    aa = [jnp.concatenate([_bf(jnp.where(incl, x[GROUP_W:, :GROUP_W], 0.0)),
                           _bf(jnp.where(incl, x[GROUP_W:, GROUP_W:], 0.0))], axis=1) for x in p]
    tinv = [_bf(x) for x in _unit_lower_inverse(n_ab, row, col)]
    tt = [jnp.concatenate([t, _bf(_mm(t, ak))], axis=1) for t, ak in zip(tinv, a_ak)]

    s0 = [s_ref[i] for i in range(n_pairs)]
    vp = [v_ref[:, s] for s in psl]
    qq = [_mm_nt(jnp.concatenate([atb[:, sl], rtb[:, sl]], axis=0), _bf(s)) for sl, s in zip(psl, s0)]
    qa = [_bf(x[:chunk]) for x in qq]
    qr = [x[chunk:] for x in qq]

    def stack_pairs(xs, q):
        return [xs[2 * q], xs[2 * q], xs[2 * q + 1], xs[2 * q + 1]]

    u_st = [_mm(tt[q], jnp.concatenate(stack_pairs(qa, q) + stack_pairs(vp, q), axis=0))
            for q in range(n_groups)]
    y_st = [jnp.concatenate(stack_pairs(qr, q), axis=0)
            + _mm(aa[q], jnp.concatenate([_bf(u_st[q])] + stack_pairs(vp, q), axis=0))
            for q in range(n_groups)]

    def unstack(x_st, i):
        j = i % 2
        lo, mid, hi = 2 * j * chunk, (2 * j + 1) * chunk, (2 * j + 2) * chunk
        return jnp.where(first_half, x_st[i // 2][lo:mid], x_st[i // 2][mid:hi])

    u = [unstack(u_st, i) for i in range(n_pairs)]
    y = [unstack(y_st, i) for i in range(n_pairs)]
    ds = [_mm_tn(jnp.concatenate([_bf(u[i]), vp[i]], axis=0),
                 jnp.concatenate([bh[:, psl[i]], kh[:, psl[i]]], axis=0)) for i in range(n_pairs)]
    for i in range(n_pairs):
        s_ref[i] = s0[i] * g_last[:, psl[i]] + jnp.where(same_head, ds[i], 0.0)

    mean = [_mm(_bf(y[i]), bd) * (1.0 / HEAD) for i in range(n_pairs)]
    dev = [y[i] - mean[i] for i in range(n_pairs)]
    var = [_mm(_bf(x * x), bd) * (1.0 / HEAD) for x in dev]
    bonus = [_mm(_bf(r[:, sl] * k[:, sl] * vec_ref[2:3, sl]), bd) for sl in psl]
    for i, sl in enumerate(psl):
        yn = dev[i] * lax.rsqrt(var[i] + LNX_EPS) * vec_ref[0:1, sl] + vec_ref[1:2, sl]
        o_ref[:, sl] = _bf((yn + bonus[i] * vp[i].astype(F32)) * g_ref[:, sl].astype(F32))


def _rwkv_scan(r, k, v, a, b, g, lw, vec, bd, batch, seq_len):
    m, d = lw.shape
    nc = seq_len // SCAN_CHUNK
    blk = pl.BlockSpec((SCAN_CHUNK, d), lambda bi, ci: (bi * nc + ci, 0))
    return pl.pallas_call(
        _rwkv_scan_kernel,
        grid=(batch, nc),
        in_specs=[blk] * 7 + [_full(vec.shape), _full(bd.shape)],
        out_specs=blk,
        out_shape=jax.ShapeDtypeStruct((m, d), BF16),
        scratch_shapes=[pltpu.VMEM((d // PAIR_W, PAIR_W, PAIR_W), F32)],
        compiler_params=_params("parallel", "arbitrary"),
        name="rwkv_scan",
    )(r, k, v, a, b, g, lw, vec, bd)


def _gelu_times(x, y):
    k1 = -2.0 * math.sqrt(2.0 / math.pi) * math.log2(math.e)
    e = jnp.exp2(x * (k1 + (k1 * 0.044715) * (x * x)))
    return (x * y) / (1.0 + e)


def _mix_ffn_kernel(tiles_per_seq, x_ref, o_ref, wp_ref, vec_ref, wg_ref, wu_ref, cw_ref, wo_ref,
                    out_ref, tail_ref):
    tm = x_ref.shape[0]
    fc = FFN_COL_BLOCK
    x1 = x_ref[...] + _rms(_mm(o_ref[...], wp_ref[...]), vec_ref[0:1, :])
    xn = _bf(_rms(x1, vec_ref[1:2, :]))

    @pl.when(pl.program_id(0) % tiles_per_seq == 0)
    def _():
        tail_ref[...] = jnp.zeros_like(tail_ref)

    row = lax.broadcasted_iota(jnp.int32, (tm, fc), 0)
    acc = jnp.zeros(x1.shape, F32)
    n_blocks = wg_ref.shape[1] // fc

    def gate_up(c):
        cs = slice(c * fc, (c + 1) * fc)
        return _mm(xn, wg_ref[:, cs]), _mm(xn, wu_ref[:, cs])

    ahead = [gate_up(c) for c in range(min(FFN_LOOKAHEAD, n_blocks))]
    for c in range(n_blocks):
        cs = slice(c * fc, (c + 1) * fc)
        gate, up = ahead.pop(0)
        if c + FFN_LOOKAHEAD < n_blocks:
            ahead.append(gate_up(c + FFN_LOOKAHEAD))
        t1 = tail_ref[SUBLANES - 1:SUBLANES, cs]
        t2 = tail_ref[SUBLANES - 2:SUBLANES - 1, cs]
        prev1 = jnp.where(row == 0, t1, pltpu.roll(gate, 1, 0))
        prev2 = jnp.where(row == 0, t2, jnp.where(row == 1, t1, pltpu.roll(gate, 2, 0)))
        tail_ref[:, cs] = gate[tm - SUBLANES:, :]
        gc = cw_ref[3:4, cs] + prev2 * cw_ref[0:1, cs] + prev1 * cw_ref[1:2, cs] + gate * cw_ref[2:3, cs]
        acc = acc + _mm(_bf(_gelu_times(gc, up)), wo_ref[cs, :])
    out_ref[...] = x1 + _rms(acc, vec_ref[2:3, :])


def _mix_ffn(x, o, seq_len, w_proj, vec, w_gate, w_up, conv, w_out):
    m, d = x.shape
    tm = FFN_ROW_TILE
    f = w_out.shape[0]
    return pl.pallas_call(
        functools.partial(_mix_ffn_kernel, seq_len // tm),
        grid=(m // tm,),
        in_specs=[pl.BlockSpec((tm, d), lambda i: (i, 0)),
                  pl.BlockSpec((tm, o.shape[1]), lambda i: (i, 0)),
                  _resident(w_proj.shape), _resident(vec.shape), _resident(w_gate.shape),
                  _resident(w_up.shape), _resident(conv.shape), _resident(w_out.shape)],
        out_specs=pl.BlockSpec((tm, d), lambda i: (i, 0)),
        out_shape=jax.ShapeDtypeStruct((m, d), F32),
        scratch_shapes=[pltpu.VMEM((SUBLANES, f), F32)],
        compiler_params=_params("arbitrary"),
        name="mix_ffn",
    )(x, o, w_proj, vec, w_gate, w_up, conv, w_out)


def _kv_prep_kernel(x_ref, g_ref, ga_ref, wd_ref, wra_ref, wrb_ref, cs_ref, wk_ref, wvt_ref, place_ref,
                    k_ref, vt_ref):
    xn = _bf(_rms(x_ref[...], g_ref[...]))
    ckv = _bf(_rms(_mm(xn, wd_ref[...]), ga_ref[...]))
    kr = _mm(xn, wra_ref[...]) * cs_ref[:, :LANES] + _mm(xn, wrb_ref[...]) * cs_ref[:, LANES:]
    k_ref[...] = _bf(_mm(ckv, wk_ref[...]) + _mm(_bf(kr), place_ref[...]))
    vt = _mm_nt(wvt_ref[...], ckv)
    ones_row = (lax.broadcasted_iota(jnp.int32, vt.shape, 0) & (LANES - 1)) == HEAD
    vt_ref[...] = _bf(jnp.where(ones_row, 1.0, vt))


def _kv_prep(x, batch, seq_len, g, ga, wd, wra, wrb, cs, wk, wvt, place):
    m, d = x.shape
    tm = ATT_TILE
    nt = seq_len // tm
    return pl.pallas_call(
        _kv_prep_kernel,
        grid=(m // tm,),
        in_specs=[pl.BlockSpec((tm, d), lambda i: (i, 0)),
                  _full(g.shape), _full(ga.shape), _full(wd.shape), _full(wra.shape), _full(wrb.shape),
                  pl.BlockSpec((tm, 2 * LANES), lambda i: (i % nt, 0)),
                  _full(wk.shape), _full(wvt.shape), _full(place.shape)],
        out_specs=[pl.BlockSpec((tm, wk.shape[1]), lambda i: (i, 0)),
                   pl.BlockSpec((None, None, wvt.shape[0], tm), lambda i: (i // nt, i % nt, 0, 0))],
        out_shape=[jax.ShapeDtypeStruct((m, wk.shape[1]), BF16),
                   jax.ShapeDtypeStruct((batch, nt, wvt.shape[0], tm), BF16)],
        compiler_params=_params("parallel"),
        name="mla_kv_prep",
    )(x, g, ga, wd, wra, wrb, cs, wk, wvt, place)


def _q_prep_kernel(scale, x_ref, g_ref, gq_ref, wd_ref, wa_ref, wb_ref, cs_ref, q_ref):
    xn = _bf(_rms(x_ref[...], g_ref[...]))
    cq = _bf(_rms(_mm(xn, wd_ref[...]), gq_ref[...]))
    qa = _mm(cq, wa_ref[...])
    qb = _mm(cq, wb_ref[...])
    cos_t = cs_ref[:, :LANES]
    sin_t = cs_ref[:, LANES:]
    for h in range(qa.shape[1] // LANES):
        sl = slice(h * LANES, (h + 1) * LANES)
        q_ref[:, sl] = _bf((qa[:, sl] * cos_t + qb[:, sl] * sin_t) * scale)


def _q_prep(x, seq_len, scale, g, gq, wd, wa, wb, cs):
    m, d = x.shape
    tm = ROW_TILE
    nt = seq_len // tm
    return pl.pallas_call(
        functools.partial(_q_prep_kernel, scale),
        grid=(m // tm,),
        in_specs=[pl.BlockSpec((tm, d), lambda i: (i, 0)),
                  _full(g.shape), _full(gq.shape), _full(wd.shape), _full(wa.shape), _full(wb.shape),
                  pl.BlockSpec((tm, 2 * LANES), lambda i: (i % nt, 0))],
        out_specs=pl.BlockSpec((tm, wa.shape[1]), lambda i: (i, 0)),
        out_shape=jax.ShapeDtypeStruct((m, wa.shape[1]), BF16),
        compiler_params=_params("parallel"),
        name="mla_q_prep",
    )(x, g, gq, wd, wa, wb, cs)


def _attn_kernel(q_ref, k_ref, vt_ref, o_ref):
    tq = q_ref.shape[0]
    tk = vt_ref.shape[-1]
    tc = tq // ATT_Q_SPLIT
    qi = pl.program_id(2)
    streams = [(h, c) for h in range(2) for c in range(ATT_Q_SPLIT)]
    hsl = [slice(h * LANES, (h + 1) * LANES) for h in range(2)]
    qs = [q_ref[c * tc:(c + 1) * tc, hsl[h]] for h, c in streams]

    def step(s, vt, m_i, acc):
        m_new = jnp.maximum(m_i, jnp.max(s, axis=0, keepdims=True))
        alpha = jnp.exp2(m_i - m_new)
        p = jnp.exp2(s - m_new)
        return m_new, alpha * acc + _mm(vt, _bf(p))

    def body(j, carry):
        off = pl.multiple_of(j * tk, tk)
        kt = [k_ref[pl.ds(off, tk), sl] for sl in hsl]
        ss = [_mm_nt(kt[h], q) for (h, _), q in zip(streams, qs)]
        return tuple(step(s, vt_ref[j, hsl[h], :], *mc) for (h, _), s, mc in zip(streams, ss, carry))

    init = tuple((jnp.full((1, tc), NEG_INF, F32), jnp.zeros((LANES, tc), F32)) for _ in streams)
    carry = lax.fori_loop(0, qi, body, init)

    off = pl.multiple_of(qi * tk, tk)
    outs = []
    kpos = lax.broadcasted_iota(jnp.int32, (tc, tc), 0) // MASK_CHUNK
    qpos = lax.broadcasted_iota(jnp.int32, (tc, tc), 1) // MASK_CHUNK
    visible = kpos <= qpos
    ss = [_mm_nt(k_ref[pl.ds(off, (c + 1) * tc), hsl[h]], q) for (h, c), q in zip(streams, qs)]
    for (h, c), s, mc in zip(streams, ss, carry):
        nk = (c + 1) * tc
        s_last = jnp.where(visible, s[nk - tc:], NEG_INF)
        s = s_last if c == 0 else jnp.concatenate([s[:nk - tc], s_last], axis=0)
        _, acc = step(s, vt_ref[qi, hsl[h], :nk], *mc)
        outs.append(acc[:HEAD] / acc[HEAD:HEAD + 1])
    o_t = jnp.concatenate([jnp.concatenate(outs[h * ATT_Q_SPLIT:(h + 1) * ATT_Q_SPLIT], axis=1)
                           for h in range(2)], axis=0)
    o_ref[...] = _bf(o_t.T)


def _attention(q, k, vt, batch, seq_len):
    m = q.shape[0]
    nkt, rows, tk = vt.shape[1:]
    n_pairs = rows // (2 * LANES)
    tq = ATT_TILE
    nq = seq_len // tq
    return pl.pallas_call(
        _attn_kernel,
        grid=(batch, n_pairs, nq),
        in_specs=[pl.BlockSpec((tq, 2 * LANES), lambda b, p, i: (b * nq + i, p)),
                  pl.BlockSpec((seq_len, 2 * LANES), lambda b, p, i: (b, p)),
                  pl.BlockSpec((None, nkt, 2 * LANES, tk), lambda b, p, i: (b, 0, p, 0))],
        out_specs=pl.BlockSpec((tq, PAIR_W), lambda b, p, i: (b * nq + i, p)),
        out_shape=jax.ShapeDtypeStruct((m, n_pairs * PAIR_W), BF16),
        compiler_params=_params("parallel", "parallel", "arbitrary"),
        name="mla_attention",
    )(q, k, vt)


def _pad_cols(w, n):
    return jnp.pad(w, ((0, 0), (0, n - w.shape[1])))


def _pad_rows(w, n):
    return jnp.pad(w, ((0, n - w.shape[0]), (0, 0)))


def _rows(vectors, d):
    rows = jnp.stack([v.reshape(d).astype(F32) for v in vectors])
    return _pad_rows(rows, -(-rows.shape[0] // SUBLANES) * SUBLANES)


def _rotate_half_cols(w):
    half = w.shape[1] // 2
    return jnp.concatenate([-w[:, half:], w[:, :half]], axis=1)


def _rope_table(seq_len):
    inv = 1.0 / (ROPE_THETA ** (jnp.arange(0, QK_ROPE, 2, dtype=F32) / QK_ROPE))
    ang = jnp.arange(seq_len, dtype=F32)[:, None] * inv[None, :]
    cos, sin = jnp.cos(ang), jnp.sin(ang)
    ones = jnp.ones((seq_len, QK_NOPE), F32)
    pad = jnp.zeros((seq_len, LANES - QK_NOPE - QK_ROPE), F32)
    cos_t = jnp.concatenate([ones, cos, cos, pad], axis=1)
    sin_t = jnp.concatenate([0 * ones, sin, sin, pad], axis=1)
    return jnp.concatenate([cos_t, sin_t], axis=1)


def kernel(x, norm_g, ffn_w_in, ffn_conv_w, ffn_conv_b, ffn_w_out, a_mu, a_w_rkv, a_w0, a_w1, a_w2, a_a0, a_a1, a_a2, a_g1, a_g2, a_k_k, a_k_a, a_r_k, a_lnx_w, a_lnx_b, a_w_o, kv_norm_g, kv_w_down, kv_a_norm_g, kv_w_up, q_w_down, q_norm_g, q_w_up, o_w):
    batch, seq_len, d = x.shape
    n_heads = d // HEAD
    n_a = a_mu.shape[0]
    depth = norm_g.shape[0]
    xf = x.reshape(batch * seq_len, d)

    head_of = jnp.arange(d) // HEAD
    e = _bf(head_of[:, None] == jnp.arange(LANES)[None, :])
    et = e.T
    pair_of = jnp.arange(PAIR_W) // HEAD
    bd = _bf(pair_of[:, None] == pair_of[None, :])
    rope_cs = _rope_table(seq_len)
    zeros_rope = jnp.zeros((1, LANES - QK_NOPE - QK_ROPE), F32)

    kq = vtq = None
    for layer in range(depth):
        gl = norm_g[layer].astype(F32)
        if layer < n_a:
            i = layer
            vec = _rows([gl[0]] + [a_mu[i, n] for n in range(6)]
                        + [a_w0[i], a_a0[i], a_k_k[i], a_k_a[i]], d)
            lora = LANES
            gate_lora = 2 * LANES
            r, k, v, a, b, g, lw = _rwkv_prep(
                xf, seq_len, vec, _bf(a_w_rkv[i]),
                _bf(_pad_cols(a_w1[i], lora)), _bf(_pad_rows(a_w2[i], lora)),
                _bf(_pad_cols(a_a1[i], lora)), _bf(_pad_rows(a_a2[i], lora)),
                _bf(_pad_cols(a_g1[i], gate_lora)), _bf(_pad_rows(a_g2[i], gate_lora)), e, et)
            svec = _rows([a_lnx_w[i], a_lnx_b[i], a_r_k[i]], d)
            mix = _rwkv_scan(r, k, v, a, b, g, lw, svec, bd, batch, seq_len)
            w_out_proj = _bf(a_w_o[i])
        else:
            if layer == n_a:
                wd = kv_w_down[:, :KV_LORA]
                wr = kv_w_down[:, KV_LORA:]
                up = kv_w_up.reshape(KV_LORA, n_heads, 2 * HEAD)
                wk = _pad_cols(up[:, :, :QK_NOPE].reshape(KV_LORA * n_heads, QK_NOPE), LANES)
                wk = wk.reshape(KV_LORA, n_heads * LANES)
                wvt = _pad_cols(up[:, :, QK_NOPE:].reshape(KV_LORA * n_heads, HEAD), LANES)
                wvt = wvt.reshape(KV_LORA, n_heads * LANES).T
                lane = jnp.arange(n_heads * LANES) % LANES
                place = (lane[None, :] == (jnp.arange(LANES) + QK_NOPE)[:, None]) & (jnp.arange(LANES) < QK_ROPE)[:, None]
                kq, vtq = _kv_prep(
                    xf, batch, seq_len, kv_norm_g.reshape(1, d), kv_a_norm_g.reshape(1, KV_LORA),
                    _bf(wd), _bf(_pad_cols(wr, LANES)), _bf(_pad_cols(_rotate_half_cols(wr), LANES)),
                    jnp.concatenate([rope_cs[:, QK_NOPE:LANES], jnp.zeros((seq_len, QK_NOPE), F32),
                                     rope_cs[:, LANES + QK_NOPE:], jnp.zeros((seq_len, QK_NOPE), F32)], axis=1),
                    _bf(wk), _bf(wvt), _bf(place))
            j = layer - n_a
            qup = q_w_up[j].reshape(-1, n_heads, QK_NOPE + QK_ROPE)
            q_lora = qup.shape[0]
            wa = _pad_cols(qup.reshape(q_lora * n_heads, -1), LANES).reshape(q_lora, n_heads * LANES)
            rot = jnp.concatenate([jnp.zeros((q_lora, n_heads, QK_NOPE), F32),
                                   jnp.concatenate([-qup[:, :, QK_NOPE + QK_ROPE // 2:],
                                                    qup[:, :, QK_NOPE:QK_NOPE + QK_ROPE // 2]], axis=-1)], axis=-1)
            wb = _pad_cols(rot.reshape(q_lora * n_heads, -1), LANES).reshape(q_lora, n_heads * LANES)
            q = _q_prep(xf, seq_len, math.log2(math.e) / math.sqrt(QK_NOPE + QK_ROPE), gl[0:1], q_norm_g[j].reshape(1, -1),
                        _bf(q_w_down[j]), _bf(wa), _bf(wb), rope_cs)
            mix = _attention(q, kq, vtq, batch, seq_len)
            w_out_proj = _bf(o_w[j])
        d_ff = ffn_conv_b.shape[1]
        conv = _rows([ffn_conv_w[layer, 0], ffn_conv_w[layer, 1], ffn_conv_w[layer, 2], ffn_conv_b[layer]], d_ff)
        w_in = _bf(ffn_w_in[layer])
        xf = _mix_ffn(xf, mix, seq_len, w_out_proj, _rows([gl[1], gl[2], gl[3]], d),
                      w_in[:, :d_ff], w_in[:, d_ff:], conv, _bf(ffn_w_out[layer]))
    return xf.reshape(batch, seq_len, d)
```

```python
import functools
import math

import jax
import jax.numpy as jnp
from jax import lax
from jax.experimental import pallas as pl
from jax.experimental.pallas import tpu as pltpu

F32 = jnp.float32
BF16 = jnp.bfloat16

HEAD = 64
LNX_EPS = 64e-5
NORM_EPS = 1e-6
QK_NOPE = 64
QK_ROPE = 32
KV_LORA = 256
ROPE_THETA = 10000.0
MASK_CHUNK = 64
NEG_INF = -1e30

LANES = 128
SUBLANES = 8
VMEM_LIMIT = 48 * 1024 * 1024

SCAN_CHUNK = 64
SCAN_BATCH = 2
ROW_TILE = 512
FFN_ROW_TILE = 512
FFN_COL_BLOCK = 256
FFN_LOOKAHEAD = 2
ATT_TILE = 1024
ATT_Q_SPLIT = 4


def _bf(x):
    return x.astype(BF16)


def _mm(a, b):
    return jnp.dot(a, b, preferred_element_type=F32)


def _mm_nt(a, b):
    return lax.dot_general(a, b, (((1,), (1,)), ((), ())), preferred_element_type=F32)


def _mm_tn(a, b):
    return lax.dot_general(a, b, (((0,), (0,)), ((), ())), preferred_element_type=F32)


def _rms(x, g):
    return x * lax.rsqrt(jnp.mean(x * x, axis=-1, keepdims=True) + NORM_EPS) * g


def _sigmoid(x):
    return 1.0 / (1.0 + jnp.exp(-x))


def _params(*sem):
    return pltpu.CompilerParams(dimension_semantics=sem, vmem_limit_bytes=VMEM_LIMIT)


def _full(shape):
    nd = len(shape)
    return pl.BlockSpec(shape, lambda *_: (0,) * nd)


def _resident(shape):
    nd = len(shape)
    return pl.BlockSpec(shape, lambda *_: (0,) * nd, pipeline_mode=pl.Buffered(1))


def _rwkv_prep_kernel(seq_len, x_ref, xh_ref, vec_ref, wrkv_ref, w1_ref, w2_ref, a1_ref, a2_ref,
                      g1_ref, g2_ref, e_ref, et_ref,
                      r_ref, k_ref, v_ref, a_ref, b_ref, g_ref, lw_ref):
    tm = x_ref.shape[0]
    gn = vec_ref[0:1, :]
    h = _rms(x_ref[...], gn)
    h_halo = _rms(xh_ref[SUBLANES - 1:SUBLANES, :], gn)
    at_start = (pl.program_id(0) * tm) % seq_len == 0
    h_halo = jnp.where(at_start, 0.0, h_halo)
    row = lax.broadcasted_iota(jnp.int32, h.shape, 0)
    h_prev = jnp.where(row == 0, h_halo, pltpu.roll(h, 1, 0))
    xx = h_prev - h

    def mix(i):
        return _bf(h + xx * vec_ref[1 + i:2 + i, :])

    r = _mm(mix(0), wrkv_ref[0])
    k = _mm(mix(1), wrkv_ref[1])
    v = _mm(mix(2), wrkv_ref[2])
    zw = vec_ref[7:8, :] + _mm(_bf(jnp.tanh(_mm(mix(3), w1_ref[...]))), w2_ref[...])
    softplus = jnp.maximum(-zw, 0.0) + jnp.log(1.0 + jnp.exp(-jnp.abs(zw)))
    lw = -jnp.exp(-softplus - 0.5)
    a = _sigmoid(vec_ref[8:9, :] + _mm(_bf(_mm(mix(4), a1_ref[...])), a2_ref[...]))
    g = _mm(_bf(_sigmoid(_mm(mix(5), g1_ref[...]))), g2_ref[...])
    kk = k * vec_ref[9:10, :]
    ss = _mm(_bf(_mm(_bf(kk * kk), e_ref[...])), et_ref[...])
    kk = kk / jnp.maximum(jnp.sqrt(ss), 1e-12)
    k = k * (1.0 + (a - 1.0) * vec_ref[10:11, :])
    r_ref[...] = _bf(r)
    k_ref[...] = _bf(k)
    v_ref[...] = _bf(v)
    a_ref[...] = _bf(-kk)
    b_ref[...] = _bf(kk * a)
    g_ref[...] = _bf(g)
    lw_ref[...] = lw


def _rwkv_prep(x, seq_len, vec, wrkv, w1, w2, a1, a2, g1, g2, e, et):
    m, d = x.shape
    tm = ROW_TILE
    halo = tm // SUBLANES
    row_spec = pl.BlockSpec((tm, d), lambda i: (i, 0))
    out_bf = jax.ShapeDtypeStruct((m, d), BF16)
    return pl.pallas_call(
        functools.partial(_rwkv_prep_kernel, seq_len),
        grid=(m // tm,),
        in_specs=[row_spec,
                  pl.BlockSpec((SUBLANES, d), lambda i: (jnp.maximum(i * halo - 1, 0), 0)),
                  _full(vec.shape), _full(wrkv.shape), _full(w1.shape), _full(w2.shape),
                  _full(a1.shape), _full(a2.shape), _full(g1.shape), _full(g2.shape),
                  _full(e.shape), _full(et.shape)],
        out_specs=[row_spec] * 7,
        out_shape=[out_bf] * 6 + [jax.ShapeDtypeStruct((m, d), F32)],
        compiler_params=_params("parallel"),
        name="rwkv_prep",
    )(x, x, vec, wrkv, w1, w2, a1, a2, g1, g2, e, et)


PAIR_W = 2 * HEAD


def _unit_lower_inverse(nms, row, col):
    diff = row ^ col
    eye = jnp.where(row == col, 1.0, 0.0)
    n8 = [jnp.where((diff >> 3) == 0, nm, 0.0) for nm in nms]
    n8b = [_bf(n) for n in n8]
    n8_2 = [_mm(n, n) for n in n8b]
    n8_2b = [_bf(n) for n in n8_2]
    n8_4 = [_bf(_mm(n, n)) for n in n8_2b]
    n8_3 = [_mm(n, n2) for n, n2 in zip(n8b, n8_2b)]
    xs = [eye + n + n2 + n3 for n, n2, n3 in zip(n8, n8_2, n8_3)]
    xs = [x + _mm(_bf(x), n4) for x, n4 in zip(xs, n8_4)]
    for shift in (3, 4, 5):
        offs = [_bf(jnp.where((diff >> shift) == 1, nm, 0.0)) for nm in nms]
        xbs = [_bf(x) for x in xs]
        ts = [_bf(_mm(xb, off)) for xb, off in zip(xbs, offs)]
        xs = [x + _mm(t, xb) for x, t, xb in zip(xs, ts, xbs)]
    return xs


def _rwkv_scan_kernel(r_ref, k_ref, v_ref, a_ref, b_ref, g_ref, lw_ref, vec_ref, bd_ref,
                      o_ref, s_ref):
    nb, chunk, d = lw_ref.shape

    @pl.when(pl.program_id(1) == 0)
    def _():
        s_ref[...] = jnp.zeros_like(s_ref)

    trow = lax.broadcasted_iota(jnp.int32, (chunk, chunk), 0)
    tcol = lax.broadcasted_iota(jnp.int32, (chunk, chunk), 1)
    tri = _bf(jnp.where(trow >= tcol, 1.0, 0.0))

    def decayed(j):
        lw = lw_ref[j]
        lw_hi = _bf(lw)
        lw_lo = _bf(lw - lw_hi.astype(F32))
        c = _mm(tri, lw_hi) + _mm(tri, lw_lo)
        c_last = c[chunk - 1:chunk, :]
        r = r_ref[j].astype(F32)
        k = k_ref[j].astype(F32)
        a = a_ref[j].astype(F32)
        b = b_ref[j].astype(F32)
        e_inv = jnp.exp(-c)
        e_rem = jnp.exp(c_last - c)
        at = a * jnp.exp(c - lw)
        rt = r * jnp.exp(c)
        return dict(r=r, k=k, at=at, rt=rt, bt=b * e_inv, kt=k * e_inv, bh=_bf(b * e_rem), kh=_bf(k * e_rem),
                    g_last=jnp.exp(c_last), atb=_bf(at), rtb=_bf(rt))

    rows = [decayed(j) for j in range(nb)]

    row = lax.broadcasted_iota(jnp.int32, (PAIR_W, PAIR_W), 0)
    col = lax.broadcasted_iota(jnp.int32, (PAIR_W, PAIR_W), 1)
    strict = (row & (HEAD - 1)) > (col & (HEAD - 1))
    incl = (row & (HEAD - 1)) >= (col & (HEAD - 1))
    same_head = (row >> 6) == (col >> 6)
    first_half = lax.broadcasted_iota(jnp.int32, (chunk, PAIR_W), 1) < HEAD
    bd = bd_ref[...]

    def stack(x):
        return jnp.concatenate([_bf(jnp.where(first_half, x, 0.0)), _bf(jnp.where(first_half, 0.0, x))], axis=0)

    n_pairs = d // PAIR_W
    items = [(j, slice(i * PAIR_W, (i + 1) * PAIR_W)) for j in range(nb) for i in range(n_pairs)]
    n_items = len(items)

    def op(name):
        return [rows[j][name][:, sl] for j, sl in items]

    lhs = [jnp.concatenate([stack(x), stack(y_)], axis=0) for x, y_ in zip(op("at"), op("rt"))]
    rhs = [jnp.concatenate([stack(x), stack(y_)], axis=0) for x, y_ in zip(op("bt"), op("kt"))]
    p = [_mm_nt(l, r_) for l, r_ in zip(lhs, rhs)]
    n_ab = [jnp.where(strict, x[:PAIR_W, :PAIR_W], 0.0) for x in p]
    a_ak = [_bf(jnp.where(strict, x[:PAIR_W, PAIR_W:], 0.0)) for x in p]
    aa = [jnp.concatenate([_bf(jnp.where(incl, x[PAIR_W:, :PAIR_W], 0.0)),
                           _bf(jnp.where(incl, x[PAIR_W:, PAIR_W:], 0.0))], axis=1) for x in p]
    tinv = [_bf(x) for x in _unit_lower_inverse(n_ab, row, col)]
    tt = [jnp.concatenate([t, _bf(_mm(t, ak))], axis=1) for t, ak in zip(tinv, a_ak)]

    s0 = [s_ref[i] for i in range(n_items)]
    vp = [v_ref[j, :, sl] for j, sl in items]
    qq = [_mm_nt(jnp.concatenate([x, y_], axis=0), _bf(s)) for x, y_, s in zip(op("atb"), op("rtb"), s0)]
    qa = [_bf(x[:chunk]) for x in qq]
    qr = [x[chunk:] for x in qq]
    u_st = [_mm(tt[i], jnp.concatenate([qa[i], qa[i], vp[i], vp[i]], axis=0)) for i in range(n_items)]
    y_st = [jnp.concatenate([qr[i], qr[i]], axis=0)
            + _mm(aa[i], jnp.concatenate([_bf(u_st[i]), vp[i], vp[i]], axis=0)) for i in range(n_items)]
    u = [jnp.where(first_half, x[:chunk], x[chunk:]) for x in u_st]
    y = [jnp.where(first_half, x[:chunk], x[chunk:]) for x in y_st]
    ds = [_mm_tn(jnp.concatenate([_bf(u[i]), vp[i]], axis=0), jnp.concatenate([x, y_], axis=0))
          for i, (x, y_) in enumerate(zip(op("bh"), op("kh")))]
    for i, g_last in enumerate(op("g_last")):
        s_ref[i] = s0[i] * g_last + jnp.where(same_head, ds[i], 0.0)

    mean = [_mm(_bf(x), bd) * (1.0 / HEAD) for x in y]
    dev = [x - m_ for x, m_ in zip(y, mean)]
    var = [_mm(_bf(x * x), bd) * (1.0 / HEAD) for x in dev]
    bonus = [_mm(_bf(r_ * k_ * vec_ref[2:3, sl]), bd) for r_, k_, (_, sl) in zip(op("r"), op("k"), items)]
    for i, (j, sl) in enumerate(items):
        yn = dev[i] * lax.rsqrt(var[i] + LNX_EPS) * vec_ref[0:1, sl] + vec_ref[1:2, sl]
        o_ref[j, :, sl] = _bf((yn + bonus[i] * vp[i].astype(F32)) * g_ref[j, :, sl].astype(F32))


def _rwkv_scan(r, k, v, a, b, g, lw, vec, bd, batch, seq_len):
    m, d = lw.shape
    nb = SCAN_BATCH
    blk = pl.BlockSpec((nb, SCAN_CHUNK, d), lambda bi, ci: (bi, ci, 0))
    seq = [x.reshape(batch, seq_len, d) for x in (r, k, v, a, b, g, lw)]
    out = pl.pallas_call(
        _rwkv_scan_kernel,
        grid=(batch // nb, seq_len // SCAN_CHUNK),
        in_specs=[blk] * 7 + [_full(vec.shape), _full(bd.shape)],
        out_specs=blk,
        out_shape=jax.ShapeDtypeStruct((batch, seq_len, d), BF16),
        scratch_shapes=[pltpu.VMEM((nb * d // PAIR_W, PAIR_W, PAIR_W), F32)],
        compiler_params=_params("parallel", "arbitrary"),
        name="rwkv_scan",
    )(*seq, vec, bd)
    return out.reshape(m, d)


def _gelu_times(x, y):
    k1 = -2.0 * math.sqrt(2.0 / math.pi) * math.log2(math.e)
    e = jnp.exp2(x * (k1 + (k1 * 0.044715) * (x * x)))
    return (x * y) / (1.0 + e)


def _mix_ffn_kernel(tiles_per_seq, x_ref, o_ref, wp_ref, vec_ref, wg_ref, wu_ref, cw_ref, wo_ref,
                    out_ref, tail_ref):
    tm = x_ref.shape[0]
    fc = FFN_COL_BLOCK
    x1 = x_ref[...] + _rms(_mm(o_ref[...], wp_ref[...]), vec_ref[0:1, :])
    xn = _bf(_rms(x1, vec_ref[1:2, :]))

    @pl.when(pl.program_id(0) % tiles_per_seq == 0)
    def _():
        tail_ref[...] = jnp.zeros_like(tail_ref)

    row = lax.broadcasted_iota(jnp.int32, (tm, fc), 0)
    acc = jnp.zeros(x1.shape, F32)
    n_blocks = wg_ref.shape[1] // fc

    def gate_up(c):
        cs = slice(c * fc, (c + 1) * fc)
        return _mm(xn, wg_ref[:, cs]), _mm(xn, wu_ref[:, cs])

    ahead = [gate_up(c) for c in range(min(FFN_LOOKAHEAD, n_blocks))]
    for c in range(n_blocks):
        cs = slice(c * fc, (c + 1) * fc)
        gate, up = ahead.pop(0)
        if c + FFN_LOOKAHEAD < n_blocks:
            ahead.append(gate_up(c + FFN_LOOKAHEAD))
        t1 = tail_ref[SUBLANES - 1:SUBLANES, cs]
        t2 = tail_ref[SUBLANES - 2:SUBLANES - 1, cs]
        prev1 = jnp.where(row == 0, t1, pltpu.roll(gate, 1, 0))
        prev2 = jnp.where(row == 0, t2, jnp.where(row == 1, t1, pltpu.roll(gate, 2, 0)))
        tail_ref[:, cs] = gate[tm - SUBLANES:, :]
        gc = cw_ref[3:4, cs] + prev2 * cw_ref[0:1, cs] + prev1 * cw_ref[1:2, cs] + gate * cw_ref[2:3, cs]
        acc = acc + _mm(_bf(_gelu_times(gc, up)), wo_ref[cs, :])
    out_ref[...] = x1 + _rms(acc, vec_ref[2:3, :])


def _mix_ffn(x, o, seq_len, w_proj, vec, w_gate, w_up, conv, w_out):
    m, d = x.shape
    tm = FFN_ROW_TILE
    f = w_out.shape[0]
    return pl.pallas_call(
        functools.partial(_mix_ffn_kernel, seq_len // tm),
        grid=(m // tm,),
        in_specs=[pl.BlockSpec((tm, d), lambda i: (i, 0)),
                  pl.BlockSpec((tm, o.shape[1]), lambda i: (i, 0)),
                  _resident(w_proj.shape), _resident(vec.shape), _resident(w_gate.shape),
                  _resident(w_up.shape), _resident(conv.shape), _resident(w_out.shape)],
        out_specs=pl.BlockSpec((tm, d), lambda i: (i, 0)),
        out_shape=jax.ShapeDtypeStruct((m, d), F32),
        scratch_shapes=[pltpu.VMEM((SUBLANES, f), F32)],
        compiler_params=_params("arbitrary"),
        name="mix_ffn",
    )(x, o, w_proj, vec, w_gate, w_up, conv, w_out)


def _kv_prep_kernel(x_ref, g_ref, ga_ref, wd_ref, wra_ref, wrb_ref, cs_ref, wk_ref, wvt_ref, place_ref,
                    k_ref, vt_ref):
    xn = _bf(_rms(x_ref[...], g_ref[...]))
    ckv = _bf(_rms(_mm(xn, wd_ref[...]), ga_ref[...]))
    kr = _mm(xn, wra_ref[...]) * cs_ref[:, :LANES] + _mm(xn, wrb_ref[...]) * cs_ref[:, LANES:]
    k_ref[...] = _bf(_mm(ckv, wk_ref[...]) + _mm(_bf(kr), place_ref[...]))
    vt = _mm_nt(wvt_ref[...], ckv)
    ones_row = (lax.broadcasted_iota(jnp.int32, vt.shape, 0) & (LANES - 1)) == HEAD
    vt_ref[...] = _bf(jnp.where(ones_row, 1.0, vt))


def _kv_prep(x, batch, seq_len, g, ga, wd, wra, wrb, cs, wk, wvt, place):
    m, d = x.shape
    tm = ATT_TILE
    nt = seq_len // tm
    return pl.pallas_call(
        _kv_prep_kernel,
        grid=(m // tm,),
        in_specs=[pl.BlockSpec((tm, d), lambda i: (i, 0)),
                  _full(g.shape), _full(ga.shape), _full(wd.shape), _full(wra.shape), _full(wrb.shape),
                  pl.BlockSpec((tm, 2 * LANES), lambda i: (i % nt, 0)),
                  _full(wk.shape), _full(wvt.shape), _full(place.shape)],
        out_specs=[pl.BlockSpec((tm, wk.shape[1]), lambda i: (i, 0)),
                   pl.BlockSpec((None, None, wvt.shape[0], tm), lambda i: (i // nt, i % nt, 0, 0))],
        out_shape=[jax.ShapeDtypeStruct((m, wk.shape[1]), BF16),
                   jax.ShapeDtypeStruct((batch, nt, wvt.shape[0], tm), BF16)],
        compiler_params=_params("parallel"),
        name="mla_kv_prep",
    )(x, g, ga, wd, wra, wrb, cs, wk, wvt, place)


def _q_prep_kernel(scale, x_ref, g_ref, gq_ref, wd_ref, wa_ref, wb_ref, cs_ref, q_ref):
    xn = _bf(_rms(x_ref[...], g_ref[...]))
    cq = _bf(_rms(_mm(xn, wd_ref[...]), gq_ref[...]))
    qa = _mm(cq, wa_ref[...])
    qb = _mm(cq, wb_ref[...])
    cos_t = cs_ref[:, :LANES]
    sin_t = cs_ref[:, LANES:]
    for h in range(qa.shape[1] // LANES):
        sl = slice(h * LANES, (h + 1) * LANES)
        q_ref[:, sl] = _bf((qa[:, sl] * cos_t + qb[:, sl] * sin_t) * scale)


def _q_prep(x, seq_len, scale, g, gq, wd, wa, wb, cs):
    m, d = x.shape
    tm = ROW_TILE
    nt = seq_len // tm
    return pl.pallas_call(
        functools.partial(_q_prep_kernel, scale),
        grid=(m // tm,),
        in_specs=[pl.BlockSpec((tm, d), lambda i: (i, 0)),
                  _full(g.shape), _full(gq.shape), _full(wd.shape), _full(wa.shape), _full(wb.shape),
                  pl.BlockSpec((tm, 2 * LANES), lambda i: (i % nt, 0))],
        out_specs=pl.BlockSpec((tm, wa.shape[1]), lambda i: (i, 0)),
        out_shape=jax.ShapeDtypeStruct((m, wa.shape[1]), BF16),
        compiler_params=_params("parallel"),
        name="mla_q_prep",
    )(x, g, gq, wd, wa, wb, cs)


def _attn_kernel(q_ref, k_ref, vt_ref, o_ref):
    tq = q_ref.shape[0]
    tk = vt_ref.shape[-1]
    tc = tq // ATT_Q_SPLIT
    qi = pl.program_id(2)
    streams = [(h, c) for h in range(2) for c in range(ATT_Q_SPLIT)]
    hsl = [slice(h * LANES, (h + 1) * LANES) for h in range(2)]
    qs = [q_ref[c * tc:(c + 1) * tc, hsl[h]] for h, c in streams]

    def step(s, vt, m_i, acc):
        m_new = jnp.maximum(m_i, jnp.max(s, axis=0, keepdims=True))
        alpha = jnp.exp2(m_i - m_new)
        p = jnp.exp2(s - m_new)
        return m_new, alpha * acc + _mm(vt, _bf(p))

    def body(j, carry):
        off = pl.multiple_of(j * tk, tk)
        kt = [k_ref[pl.ds(off, tk), sl] for sl in hsl]
        ss = [_mm_nt(kt[h], q) for (h, _), q in zip(streams, qs)]
        return tuple(step(s, vt_ref[j, hsl[h], :], *mc) for (h, _), s, mc in zip(streams, ss, carry))

    init = tuple((jnp.full((1, tc), NEG_INF, F32), jnp.zeros((LANES, tc), F32)) for _ in streams)
    carry = lax.fori_loop(0, qi, body, init)

    off = pl.multiple_of(qi * tk, tk)
    outs = []
    kpos = lax.broadcasted_iota(jnp.int32, (tc, tc), 0) // MASK_CHUNK
    qpos = lax.broadcasted_iota(jnp.int32, (tc, tc), 1) // MASK_CHUNK
    visible = kpos <= qpos
    ss = [_mm_nt(k_ref[pl.ds(off, (c + 1) * tc), hsl[h]], q) for (h, c), q in zip(streams, qs)]
    for (h, c), s, mc in zip(streams, ss, carry):
        nk = (c + 1) * tc
        s_last = jnp.where(visible, s[nk - tc:], NEG_INF)
        s = s_last if c == 0 else jnp.concatenate([s[:nk - tc], s_last], axis=0)
        _, acc = step(s, vt_ref[qi, hsl[h], :nk], *mc)
        outs.append(acc[:HEAD] / acc[HEAD:HEAD + 1])
    o_t = jnp.concatenate([jnp.concatenate(outs[h * ATT_Q_SPLIT:(h + 1) * ATT_Q_SPLIT], axis=1)
                           for h in range(2)], axis=0)
    o_ref[...] = _bf(o_t.T)


def _attention(q, k, vt, batch, seq_len):
    m = q.shape[0]
    nkt, rows, tk = vt.shape[1:]
    n_pairs = rows // (2 * LANES)
    tq = ATT_TILE
    nq = seq_len // tq
    return pl.pallas_call(
        _attn_kernel,
        grid=(batch, n_pairs, nq),
        in_specs=[pl.BlockSpec((tq, 2 * LANES), lambda b, p, i: (b * nq + i, p)),
                  pl.BlockSpec((seq_len, 2 * LANES), lambda b, p, i: (b, p)),
                  pl.BlockSpec((None, nkt, 2 * LANES, tk), lambda b, p, i: (b, 0, p, 0))],
        out_specs=pl.BlockSpec((tq, PAIR_W), lambda b, p, i: (b * nq + i, p)),
        out_shape=jax.ShapeDtypeStruct((m, n_pairs * PAIR_W), BF16),
        compiler_params=_params("parallel", "parallel", "arbitrary"),
        name="mla_attention",
    )(q, k, vt)


def _pad_cols(w, n):
    return jnp.pad(w, ((0, 0), (0, n - w.shape[1])))


def _pad_rows(w, n):
    return jnp.pad(w, ((0, n - w.shape[0]), (0, 0)))


def _rows(vectors, d):
    rows = jnp.stack([v.reshape(d).astype(F32) for v in vectors])
    return _pad_rows(rows, -(-rows.shape[0] // SUBLANES) * SUBLANES)


def _rotate_half_cols(w):
    half = w.shape[1] // 2
    return jnp.concatenate([-w[:, half:], w[:, :half]], axis=1)


def _rope_table(seq_len):
    inv = 1.0 / (ROPE_THETA ** (jnp.arange(0, QK_ROPE, 2, dtype=F32) / QK_ROPE))
    ang = jnp.arange(seq_len, dtype=F32)[:, None] * inv[None, :]
    cos, sin = jnp.cos(ang), jnp.sin(ang)
    ones = jnp.ones((seq_len, QK_NOPE), F32)
    pad = jnp.zeros((seq_len, LANES - QK_NOPE - QK_ROPE), F32)
    cos_t = jnp.concatenate([ones, cos, cos, pad], axis=1)
    sin_t = jnp.concatenate([0 * ones, sin, sin, pad], axis=1)
    return jnp.concatenate([cos_t, sin_t], axis=1)


def kernel(x, norm_g, ffn_w_in, ffn_conv_w, ffn_conv_b, ffn_w_out, a_mu, a_w_rkv, a_w0, a_w1, a_w2, a_a0, a_a1, a_a2, a_g1, a_g2, a_k_k, a_k_a, a_r_k, a_lnx_w, a_lnx_b, a_w_o, kv_norm_g, kv_w_down, kv_a_norm_g, kv_w_up, q_w_down, q_norm_g, q_w_up, o_w):
    batch, seq_len, d = x.shape
    n_heads = d // HEAD
    n_a = a_mu.shape[0]
    depth = norm_g.shape[0]
    xf = x.reshape(batch * seq_len, d)

    head_of = jnp.arange(d) // HEAD
    e = _bf(head_of[:, None] == jnp.arange(LANES)[None, :])
    et = e.T
    pair_of = jnp.arange(PAIR_W) // HEAD
    bd = _bf(pair_of[:, None] == pair_of[None, :])
    rope_cs = _rope_table(seq_len)
    zeros_rope = jnp.zeros((1, LANES - QK_NOPE - QK_ROPE), F32)

    kq = vtq = None
    for layer in range(depth):
        gl = norm_g[layer].astype(F32)
        if layer < n_a:
            i = layer
            vec = _rows([gl[0]] + [a_mu[i, n] for n in range(6)]
                        + [a_w0[i], a_a0[i], a_k_k[i], a_k_a[i]], d)
            lora = LANES
            gate_lora = 2 * LANES
            r, k, v, a, b, g, lw = _rwkv_prep(
                xf, seq_len, vec, _bf(a_w_rkv[i]),
                _bf(_pad_cols(a_w1[i], lora)), _bf(_pad_rows(a_w2[i], lora)),
                _bf(_pad_cols(a_a1[i], lora)), _bf(_pad_rows(a_a2[i], lora)),
                _bf(_pad_cols(a_g1[i], gate_lora)), _bf(_pad_rows(a_g2[i], gate_lora)), e, et)
            svec = _rows([a_lnx_w[i], a_lnx_b[i], a_r_k[i]], d)
            mix = _rwkv_scan(r, k, v, a, b, g, lw, svec, bd, batch, seq_len)
            w_out_proj = _bf(a_w_o[i])
        else:
            if layer == n_a:
                wd = kv_w_down[:, :KV_LORA]
                wr = kv_w_down[:, KV_LORA:]
                up = kv_w_up.reshape(KV_LORA, n_heads, 2 * HEAD)
                wk = _pad_cols(up[:, :, :QK_NOPE].reshape(KV_LORA * n_heads, QK_NOPE), LANES)
                wk = wk.reshape(KV_LORA, n_heads * LANES)
                wvt = _pad_cols(up[:, :, QK_NOPE:].reshape(KV_LORA * n_heads, HEAD), LANES)
                wvt = wvt.reshape(KV_LORA, n_heads * LANES).T
                lane = jnp.arange(n_heads * LANES) % LANES
                place = (lane[None, :] == (jnp.arange(LANES) + QK_NOPE)[:, None]) & (jnp.arange(LANES) < QK_ROPE)[:, None]
                kq, vtq = _kv_prep(
                    xf, batch, seq_len, kv_norm_g.reshape(1, d), kv_a_norm_g.reshape(1, KV_LORA),
                    _bf(wd), _bf(_pad_cols(wr, LANES)), _bf(_pad_cols(_rotate_half_cols(wr), LANES)),
                    jnp.concatenate([rope_cs[:, QK_NOPE:LANES], jnp.zeros((seq_len, QK_NOPE), F32),
                                     rope_cs[:, LANES + QK_NOPE:], jnp.zeros((seq_len, QK_NOPE), F32)], axis=1),
                    _bf(wk), _bf(wvt), _bf(place))
            j = layer - n_a
            qup = q_w_up[j].reshape(-1, n_heads, QK_NOPE + QK_ROPE)
            q_lora = qup.shape[0]
            wa = _pad_cols(qup.reshape(q_lora * n_heads, -1), LANES).reshape(q_lora, n_heads * LANES)
            rot = jnp.concatenate([jnp.zeros((q_lora, n_heads, QK_NOPE), F32),
                                   jnp.concatenate([-qup[:, :, QK_NOPE + QK_ROPE // 2:],
                                                    qup[:, :, QK_NOPE:QK_NOPE + QK_ROPE // 2]], axis=-1)], axis=-1)
            wb = _pad_cols(rot.reshape(q_lora * n_heads, -1), LANES).reshape(q_lora, n_heads * LANES)
            q = _q_prep(xf, seq_len, math.log2(math.e) / math.sqrt(QK_NOPE + QK_ROPE), gl[0:1], q_norm_g[j].reshape(1, -1),
                        _bf(q_w_down[j]), _bf(wa), _bf(wb), rope_cs)
            mix = _attention(q, kq, vtq, batch, seq_len)
            w_out_proj = _bf(o_w[j])
        d_ff = ffn_conv_b.shape[1]
        conv = _rows([ffn_conv_w[layer, 0], ffn_conv_w[layer, 1], ffn_conv_w[layer, 2], ffn_conv_b[layer]], d_ff)
        w_in = _bf(ffn_w_in[layer])
        xf = _mix_ffn(xf, mix, seq_len, w_out_proj, _rows([gl[1], gl[2], gl[3]], d),
                      w_in[:, :d_ff], w_in[:, d_ff:], conv, _bf(ffn_w_out[layer]))
    return xf.reshape(batch, seq_len, d)
```

```python
import functools
import math

import jax
import jax.numpy as jnp
from jax import lax
from jax.experimental import pallas as pl
from jax.experimental.pallas import tpu as pltpu

F32 = jnp.float32
BF16 = jnp.bfloat16

HEAD = 64
LNX_EPS = 64e-5
NORM_EPS = 1e-6
QK_NOPE = 64
QK_ROPE = 32
KV_LORA = 256
ROPE_THETA = 10000.0
MASK_CHUNK = 64
NEG_INF = -1e30

LANES = 128
SUBLANES = 8
VMEM_LIMIT = 48 * 1024 * 1024

SCAN_CHUNK = 64
SCAN_BATCH = 2
ROW_TILE = 512
FFN_ROW_TILE = 512
FFN_COL_BLOCK = 256
FFN_LOOKAHEAD = 2
FFN_OUT_GROUP = 6
ATT_TILE = 1024
ATT_Q_SPLIT = 4


def _bf(x):
    return x.astype(BF16)


def _mm(a, b):
    return jnp.dot(a, b, preferred_element_type=F32)


def _mm_nt(a, b):
    return lax.dot_general(a, b, (((1,), (1,)), ((), ())), preferred_element_type=F32)


def _mm_tn(a, b):
    return lax.dot_general(a, b, (((0,), (0,)), ((), ())), preferred_element_type=F32)


def _rms(x, g):
    return x * lax.rsqrt(jnp.mean(x * x, axis=-1, keepdims=True) + NORM_EPS) * g


def _sigmoid(x):
    return 1.0 / (1.0 + jnp.exp2(x * -math.log2(math.e)))


def _params(*sem):
    return pltpu.CompilerParams(dimension_semantics=sem, vmem_limit_bytes=VMEM_LIMIT)


def _full(shape):
    nd = len(shape)
    return pl.BlockSpec(shape, lambda *_: (0,) * nd)


def _resident(shape):
    nd = len(shape)
    return pl.BlockSpec(shape, lambda *_: (0,) * nd, pipeline_mode=pl.Buffered(1))


def _rwkv_prep_kernel(seq_len, x_ref, xh_ref, vec_ref, wrkv_ref, w1_ref, w2_ref, a1_ref, a2_ref,
                      g1_ref, g2_ref, e_ref, et_ref,
                      r_ref, k_ref, v_ref, a_ref, b_ref, g_ref, lw_ref, h_ref, xx_ref):
    tm = x_ref.shape[0]
    gn = vec_ref[0:1, :]
    h = _rms(x_ref[...], gn)
    h_halo = _rms(xh_ref[SUBLANES - 1:SUBLANES, :], gn)
    at_start = (pl.program_id(0) * tm) % seq_len == 0
    h_halo = jnp.where(at_start, 0.0, h_halo)
    row = lax.broadcasted_iota(jnp.int32, h.shape, 0)
    h_prev = jnp.where(row == 0, h_halo, pltpu.roll(h, 1, 0))
    h_ref[...] = h
    xx_ref[...] = h_prev - h

    def mix(i):
        return _bf(h_ref[...] + xx_ref[...] * vec_ref[1 + i:2 + i, :])

    r = _mm(mix(0), wrkv_ref[0])
    k = _mm(mix(1), wrkv_ref[1])
    v = _mm(mix(2), wrkv_ref[2])
    zw = vec_ref[7:8, :] + _mm(_bf(jnp.tanh(_mm(mix(3), w1_ref[...]))), w2_ref[...])
    lw = -math.exp(-0.5) * _sigmoid(zw)
    a = _sigmoid(vec_ref[8:9, :] + _mm(_bf(_mm(mix(4), a1_ref[...])), a2_ref[...]))
    g = _mm(_bf(_sigmoid(_mm(mix(5), g1_ref[...]))), g2_ref[...])
    kk = k * vec_ref[9:10, :]
    ss = _mm(_bf(_mm(_bf(kk * kk), e_ref[...])), et_ref[...])
    kk = kk * lax.rsqrt(jnp.maximum(ss, 1e-24))
    k = k * (1.0 + (a - 1.0) * vec_ref[10:11, :])
    r_ref[...] = _bf(r)
    k_ref[...] = _bf(k)
    v_ref[...] = _bf(v)
    a_ref[...] = _bf(-kk)
    b_ref[...] = _bf(kk * a)
    g_ref[...] = _bf(g)
    lw_ref[...] = lw


def _rwkv_prep(x, seq_len, vec, wrkv, w1, w2, a1, a2, g1, g2, e, et):
    m, d = x.shape
    tm = ROW_TILE
    halo = tm // SUBLANES
    row_spec = pl.BlockSpec((tm, d), lambda i: (i, 0))
    out_bf = jax.ShapeDtypeStruct((m, d), BF16)
    return pl.pallas_call(
        functools.partial(_rwkv_prep_kernel, seq_len),
        grid=(m // tm,),
        in_specs=[row_spec,
                  pl.BlockSpec((SUBLANES, d), lambda i: (jnp.maximum(i * halo - 1, 0), 0)),
                  _full(vec.shape), _full(wrkv.shape), _full(w1.shape), _full(w2.shape),
                  _full(a1.shape), _full(a2.shape), _full(g1.shape), _full(g2.shape),
                  _full(e.shape), _full(et.shape)],
        out_specs=[row_spec] * 7,
        out_shape=[out_bf] * 6 + [jax.ShapeDtypeStruct((m, d), F32)],
        scratch_shapes=[pltpu.VMEM((tm, d), F32), pltpu.VMEM((tm, d), F32)],
        compiler_params=_params("parallel"),
        name="rwkv_prep",
    )(x, x, vec, wrkv, w1, w2, a1, a2, g1, g2, e, et)


PAIR_W = 2 * HEAD


def _unit_lower_inverse(nms, row, col):
    diff = row ^ col
    eye = jnp.where(row == col, 1.0, 0.0)
    n8 = [jnp.where((diff >> 3) == 0, nm, 0.0) for nm in nms]
    n8b = [_bf(n) for n in n8]
    n8_2 = [_mm(n, n) for n in n8b]
    n8_2b = [_bf(n) for n in n8_2]
    n8_4 = [_bf(_mm(n, n)) for n in n8_2b]
    n8_3 = [_mm(n, n2) for n, n2 in zip(n8b, n8_2b)]
    xs = [eye + n + n2 + n3 for n, n2, n3 in zip(n8, n8_2, n8_3)]
    xs = [x + _mm(_bf(x), n4) for x, n4 in zip(xs, n8_4)]
    for shift in (3, 4, 5):
        offs = [_bf(jnp.where((diff >> shift) == 1, nm, 0.0)) for nm in nms]
        xbs = [_bf(x) for x in xs]
        ts = [_bf(_mm(xb, off)) for xb, off in zip(xbs, offs)]
        xs = [x + _mm(t, xb) for x, t, xb in zip(xs, ts, xbs)]
    return xs


def _rwkv_scan_kernel(r_ref, k_ref, v_ref, a_ref, b_ref, g_ref, lw_ref, vec_ref, bd_ref,
                      o_ref, s_ref):
    nb, chunk, d = lw_ref.shape

    @pl.when(pl.program_id(1) == 0)
    def _():
        s_ref[...] = jnp.zeros_like(s_ref)

    trow = lax.broadcasted_iota(jnp.int32, (chunk, chunk), 0)
    tcol = lax.broadcasted_iota(jnp.int32, (chunk, chunk), 1)
    tri = _bf(jnp.where(trow >= tcol, 1.0, 0.0))

    def decayed(j):
        lw = lw_ref[j]
        lw_hi = _bf(lw)
        lw_lo = _bf(lw - lw_hi.astype(F32))
        c = _mm(tri, lw_hi) + _mm(tri, lw_lo)
        c_last = c[chunk - 1:chunk, :]
        r = r_ref[j].astype(F32)
        k = k_ref[j].astype(F32)
        a = a_ref[j].astype(F32)
        b = b_ref[j].astype(F32)
        e_inv = jnp.exp(-c)
        e_rem = jnp.exp(c_last - c)
        at = a * jnp.exp(c - lw)
        rt = r * jnp.exp(c)
        return dict(r=r, k=k, at=at, rt=rt, bt=b * e_inv, kt=k * e_inv, bh=_bf(b * e_rem), kh=_bf(k * e_rem),
                    g_last=jnp.exp(c_last), atb=_bf(at), rtb=_bf(rt))

    rows = [decayed(j) for j in range(nb)]

    row = lax.broadcasted_iota(jnp.int32, (PAIR_W, PAIR_W), 0)
    col = lax.broadcasted_iota(jnp.int32, (PAIR_W, PAIR_W), 1)
    strict = (row & (HEAD - 1)) > (col & (HEAD - 1))
    incl = (row & (HEAD - 1)) >= (col & (HEAD - 1))
    same_head = (row >> 6) == (col >> 6)
    first_half = lax.broadcasted_iota(jnp.int32, (chunk, PAIR_W), 1) < HEAD
    bd = bd_ref[...]

    def stack(x):
        return jnp.concatenate([_bf(jnp.where(first_half, x, 0.0)), _bf(jnp.where(first_half, 0.0, x))], axis=0)

    n_pairs = d // PAIR_W
    items = [(j, slice(i * PAIR_W, (i + 1) * PAIR_W)) for j in range(nb) for i in range(n_pairs)]
    n_items = len(items)

    def op(name):
        return [rows[j][name][:, sl] for j, sl in items]

    lhs = [jnp.concatenate([stack(x), stack(y_)], axis=0) for x, y_ in zip(op("at"), op("rt"))]
    rhs = [jnp.concatenate([stack(x), stack(y_)], axis=0) for x, y_ in zip(op("bt"), op("kt"))]
    p = [_mm_nt(l, r_) for l, r_ in zip(lhs, rhs)]
    n_ab = [jnp.where(strict, x[:PAIR_W, :PAIR_W], 0.0) for x in p]
    a_ak = [_bf(jnp.where(strict, x[:PAIR_W, PAIR_W:], 0.0)) for x in p]
    aa = [jnp.concatenate([_bf(jnp.where(incl, x[PAIR_W:, :PAIR_W], 0.0)),
                           _bf(jnp.where(incl, x[PAIR_W:, PAIR_W:], 0.0))], axis=1) for x in p]
    tinv = [_bf(x) for x in _unit_lower_inverse(n_ab, row, col)]
    tt = [jnp.concatenate([t, _bf(_mm(t, ak))], axis=1) for t, ak in zip(tinv, a_ak)]

    s0 = [s_ref[i] for i in range(n_items)]
    vp = [v_ref[j, :, sl] for j, sl in items]
    qq = [_mm_nt(jnp.concatenate([x, y_], axis=0), _bf(s)) for x, y_, s in zip(op("atb"), op("rtb"), s0)]
    qa = [_bf(x[:chunk]) for x in qq]
    qr = [x[chunk:] for x in qq]
    u_st = [_mm(tt[i], jnp.concatenate([qa[i], qa[i], vp[i], vp[i]], axis=0)) for i in range(n_items)]
    y_st = [jnp.concatenate([qr[i], qr[i]], axis=0)
            + _mm(aa[i], jnp.concatenate([_bf(u_st[i]), vp[i], vp[i]], axis=0)) for i in range(n_items)]
    u = [jnp.where(first_half, x[:chunk], x[chunk:]) for x in u_st]
    y = [jnp.where(first_half, x[:chunk], x[chunk:]) for x in y_st]
    ds = [_mm_tn(jnp.concatenate([_bf(u[i]), vp[i]], axis=0), jnp.concatenate([x, y_], axis=0))
          for i, (x, y_) in enumerate(zip(op("bh"), op("kh")))]
    for i, g_last in enumerate(op("g_last")):
        s_ref[i] = s0[i] * g_last + jnp.where(same_head, ds[i], 0.0)

    mean = [_mm(_bf(x), bd) * (1.0 / HEAD) for x in y]
    dev = [x - m_ for x, m_ in zip(y, mean)]
    var = [_mm(_bf(x * x), bd) * (1.0 / HEAD) for x in dev]
    bonus = [_mm(_bf(r_ * k_ * vec_ref[2:3, sl]), bd) for r_, k_, (_, sl) in zip(op("r"), op("k"), items)]
    for i, (j, sl) in enumerate(items):
        yn = dev[i] * lax.rsqrt(var[i] + LNX_EPS) * vec_ref[0:1, sl] + vec_ref[1:2, sl]
        o_ref[j, :, sl] = _bf((yn + bonus[i] * vp[i].astype(F32)) * g_ref[j, :, sl].astype(F32))


def _rwkv_scan(r, k, v, a, b, g, lw, vec, bd, batch, seq_len):
    m, d = lw.shape
    nb = SCAN_BATCH
    blk = pl.BlockSpec((nb, SCAN_CHUNK, d), lambda bi, ci: (bi, ci, 0))
    seq = [x.reshape(batch, seq_len, d) for x in (r, k, v, a, b, g, lw)]
    out = pl.pallas_call(
        _rwkv_scan_kernel,
        grid=(batch // nb, seq_len // SCAN_CHUNK),
        in_specs=[blk] * 7 + [_full(vec.shape), _full(bd.shape)],
        out_specs=blk,
        out_shape=jax.ShapeDtypeStruct((batch, seq_len, d), BF16),
        scratch_shapes=[pltpu.VMEM((nb * d // PAIR_W, PAIR_W, PAIR_W), F32)],
        compiler_params=_params("parallel", "arbitrary"),
        name="rwkv_scan",
    )(*seq, vec, bd)
    return out.reshape(m, d)


def _gelu_times(x, y):
    k1 = -2.0 * math.sqrt(2.0 / math.pi) * math.log2(math.e)
    e = jnp.exp2(x * (k1 + (k1 * 0.044715) * (x * x)))
    return (x * y) / (1.0 + e)


def _mix_ffn_kernel(tiles_per_seq, x_ref, o_ref, wp_ref, vec_ref, wgu_ref, cw_ref, wo_ref,
                    out_ref, tail_ref):
    tm = x_ref.shape[0]
    fc = FFN_COL_BLOCK
    x1 = x_ref[...] + _rms(_mm(o_ref[...], wp_ref[...]), vec_ref[0:1, :])
    xn = _bf(_rms(x1, vec_ref[1:2, :]))

    @pl.when(pl.program_id(0) % tiles_per_seq == 0)
    def _():
        tail_ref[...] = jnp.zeros_like(tail_ref)

    row = lax.broadcasted_iota(jnp.int32, (tm, fc), 0)
    acc = jnp.zeros(x1.shape, F32)
    n_blocks = wgu_ref.shape[1] // (2 * fc)

    def gate_up(c):
        gu = _mm(xn, wgu_ref[:, 2 * c * fc:2 * (c + 1) * fc])
        return gu[:, :fc], gu[:, fc:]

    ahead = [gate_up(c) for c in range(min(FFN_LOOKAHEAD, n_blocks))]
    hidden = []
    for c in range(n_blocks):
        cs = slice(c * fc, (c + 1) * fc)
        gate, up = ahead.pop(0)
        if c + FFN_LOOKAHEAD < n_blocks:
            ahead.append(gate_up(c + FFN_LOOKAHEAD))
        t1 = tail_ref[SUBLANES - 1:SUBLANES, cs]
        t2 = tail_ref[SUBLANES - 2:SUBLANES - 1, cs]
        prev1 = jnp.where(row == 0, t1, pltpu.roll(gate, 1, 0))
        prev2 = jnp.where(row == 0, t2, jnp.where(row == 1, t1, pltpu.roll(gate, 2, 0)))
        tail_ref[:, cs] = gate[tm - SUBLANES:, :]
        gc = cw_ref[3:4, cs] + prev2 * cw_ref[0:1, cs] + prev1 * cw_ref[1:2, cs] + gate * cw_ref[2:3, cs]
        hidden.append(_bf(_gelu_times(gc, up)))
        if len(hidden) == FFN_OUT_GROUP or c == n_blocks - 1:
            lo = (c + 1 - len(hidden)) * fc
            acc = acc + _mm(jnp.concatenate(hidden, axis=1), wo_ref[lo:(c + 1) * fc, :])
            hidden = []
    out_ref[...] = x1 + _rms(acc, vec_ref[2:3, :])


def _mix_ffn(x, o, seq_len, w_proj, vec, w_in, conv, w_out):
    m, d = x.shape
    tm = FFN_ROW_TILE
    f = w_out.shape[0]
    n_blocks = f // FFN_COL_BLOCK
    w_gu = w_in.reshape(d, 2, n_blocks, FFN_COL_BLOCK).transpose(0, 2, 1, 3).reshape(d, 2 * f)
    return pl.pallas_call(
        functools.partial(_mix_ffn_kernel, seq_len // tm),
        grid=(m // tm,),
        in_specs=[pl.BlockSpec((tm, d), lambda i: (i, 0)),
                  pl.BlockSpec((tm, o.shape[1]), lambda i: (i, 0)),
                  _resident(w_proj.shape), _resident(vec.shape), _resident(w_gu.shape),
                  _resident(conv.shape), _resident(w_out.shape)],
        out_specs=pl.BlockSpec((tm, d), lambda i: (i, 0)),
        out_shape=jax.ShapeDtypeStruct((m, d), F32),
        scratch_shapes=[pltpu.VMEM((SUBLANES, f), F32)],
        compiler_params=_params("arbitrary"),
        name="mix_ffn",
    )(x, o, w_proj, vec, w_gu, conv, w_out)


def _kv_prep_kernel(x_ref, g_ref, ga_ref, wd_ref, wra_ref, wrb_ref, cs_ref, wk_ref, wvt_ref, place_ref,
                    k_ref, vt_ref):
    xn = _bf(_rms(x_ref[...], g_ref[...]))
    ckv = _bf(_rms(_mm(xn, wd_ref[...]), ga_ref[...]))
    kr = _mm(xn, wra_ref[...]) * cs_ref[:, :LANES] + _mm(xn, wrb_ref[...]) * cs_ref[:, LANES:]
    k_ref[...] = _bf(_mm(ckv, wk_ref[...]) + _mm(_bf(kr), place_ref[...]))
    vt = _mm_nt(wvt_ref[...], ckv)
    ones_row = (lax.broadcasted_iota(jnp.int32, vt.shape, 0) & (LANES - 1)) == HEAD
    vt_ref[...] = _bf(jnp.where(ones_row, 1.0, vt))


def _kv_prep(x, batch, seq_len, g, ga, wd, wra, wrb, cs, wk, wvt, place):
    m, d = x.shape
    tm = ATT_TILE
    nt = seq_len // tm
    return pl.pallas_call(
        _kv_prep_kernel,
        grid=(m // tm,),
        in_specs=[pl.BlockSpec((tm, d), lambda i: (i, 0)),
                  _full(g.shape), _full(ga.shape), _full(wd.shape), _full(wra.shape), _full(wrb.shape),
                  pl.BlockSpec((tm, 2 * LANES), lambda i: (i % nt, 0)),
                  _full(wk.shape), _full(wvt.shape), _full(place.shape)],
        out_specs=[pl.BlockSpec((tm, wk.shape[1]), lambda i: (i, 0)),
                   pl.BlockSpec((None, None, wvt.shape[0], tm), lambda i: (i // nt, i % nt, 0, 0))],
        out_shape=[jax.ShapeDtypeStruct((m, wk.shape[1]), BF16),
                   jax.ShapeDtypeStruct((batch, nt, wvt.shape[0], tm), BF16)],
        compiler_params=_params("parallel"),
        name="mla_kv_prep",
    )(x, g, ga, wd, wra, wrb, cs, wk, wvt, place)


def _q_prep_kernel(scale, x_ref, g_ref, gq_ref, wd_ref, wa_ref, wb_ref, cs_ref, q_ref):
    xn = _bf(_rms(x_ref[...], g_ref[...]))
    cq = _bf(_rms(_mm(xn, wd_ref[...]), gq_ref[...]))
    qa = _mm(cq, wa_ref[...])
    qb = _mm(cq, wb_ref[...])
    cos_t = cs_ref[:, :LANES]
    sin_t = cs_ref[:, LANES:]
    for h in range(qa.shape[1] // LANES):
        sl = slice(h * LANES, (h + 1) * LANES)
        q_ref[:, sl] = _bf((qa[:, sl] * cos_t + qb[:, sl] * sin_t) * scale)


def _q_prep(x, seq_len, scale, g, gq, wd, wa, wb, cs):
    m, d = x.shape
    tm = ROW_TILE
    nt = seq_len // tm
    return pl.pallas_call(
        functools.partial(_q_prep_kernel, scale),
        grid=(m // tm,),
        in_specs=[pl.BlockSpec((tm, d), lambda i: (i, 0)),
                  _full(g.shape), _full(gq.shape), _full(wd.shape), _full(wa.shape), _full(wb.shape),
                  pl.BlockSpec((tm, 2 * LANES), lambda i: (i % nt, 0))],
        out_specs=pl.BlockSpec((tm, wa.shape[1]), lambda i: (i, 0)),
        out_shape=jax.ShapeDtypeStruct((m, wa.shape[1]), BF16),
        compiler_params=_params("parallel"),
        name="mla_q_prep",
    )(x, g, gq, wd, wa, wb, cs)


def _attn_kernel(q_ref, k_ref, vt_ref, o_ref):
    tq = q_ref.shape[0]
    tk = vt_ref.shape[-1]
    tc = tq // ATT_Q_SPLIT
    qi = pl.program_id(2)
    streams = [(h, c) for h in range(2) for c in range(ATT_Q_SPLIT)]
    hsl = [slice(h * LANES, (h + 1) * LANES) for h in range(2)]
    qs = [q_ref[c * tc:(c + 1) * tc, hsl[h]] for h, c in streams]

    def step(s, vt, m_i, acc):
        m_new = jnp.maximum(m_i, jnp.max(s, axis=0, keepdims=True))
        alpha = jnp.exp2(m_i - m_new)
        p = jnp.exp2(s - m_new)
        return m_new, alpha * acc + _mm(vt, _bf(p))

    def body(j, carry):
        off = pl.multiple_of(j * tk, tk)
        kt = [k_ref[pl.ds(off, tk), sl] for sl in hsl]
        ss = [_mm_nt(kt[h], q) for (h, _), q in zip(streams, qs)]
        return tuple(step(s, vt_ref[j, hsl[h], :], *mc) for (h, _), s, mc in zip(streams, ss, carry))

    init = tuple((jnp.full((1, tc), NEG_INF, F32), jnp.zeros((LANES, tc), F32)) for _ in streams)
    carry = lax.fori_loop(0, qi, body, init)

    off = pl.multiple_of(qi * tk, tk)
    outs = []
    kpos = lax.broadcasted_iota(jnp.int32, (tc, tc), 0) // MASK_CHUNK
    qpos = lax.broadcasted_iota(jnp.int32, (tc, tc), 1) // MASK_CHUNK
    visible = kpos <= qpos
    ss = [_mm_nt(k_ref[pl.ds(off, (c + 1) * tc), hsl[h]], q) for (h, c), q in zip(streams, qs)]
    for (h, c), s, mc in zip(streams, ss, carry):
        nk = (c + 1) * tc
        s_last = jnp.where(visible, s[nk - tc:], NEG_INF)
        s = s_last if c == 0 else jnp.concatenate([s[:nk - tc], s_last], axis=0)
        _, acc = step(s, vt_ref[qi, hsl[h], :nk], *mc)
        outs.append(acc[:HEAD] / acc[HEAD:HEAD + 1])
    o_t = jnp.concatenate([jnp.concatenate(outs[h * ATT_Q_SPLIT:(h + 1) * ATT_Q_SPLIT], axis=1)
                           for h in range(2)], axis=0)
    o_ref[...] = _bf(o_t.T)


def _attention(q, k, vt, batch, seq_len):
    m = q.shape[0]
    nkt, rows, tk = vt.shape[1:]
    n_pairs = rows // (2 * LANES)
    tq = ATT_TILE
    nq = seq_len // tq
    return pl.pallas_call(
        _attn_kernel,
        grid=(batch, n_pairs, nq),
        in_specs=[pl.BlockSpec((tq, 2 * LANES), lambda b, p, i: (b * nq + i, p)),
                  pl.BlockSpec((seq_len, 2 * LANES), lambda b, p, i: (b, p)),
                  pl.BlockSpec((None, nkt, 2 * LANES, tk), lambda b, p, i: (b, 0, p, 0))],
        out_specs=pl.BlockSpec((tq, PAIR_W), lambda b, p, i: (b * nq + i, p)),
        out_shape=jax.ShapeDtypeStruct((m, n_pairs * PAIR_W), BF16),
        compiler_params=_params("parallel", "parallel", "arbitrary"),
        name="mla_attention",
    )(q, k, vt)


def _pad_cols(w, n):
    return jnp.pad(w, ((0, 0), (0, n - w.shape[1])))


def _pad_rows(w, n):
    return jnp.pad(w, ((0, n - w.shape[0]), (0, 0)))


def _rows(vectors, d):
    rows = jnp.stack([v.reshape(d).astype(F32) for v in vectors])
    return _pad_rows(rows, -(-rows.shape[0] // SUBLANES) * SUBLANES)


def _rotate_half_cols(w):
    half = w.shape[1] // 2
    return jnp.concatenate([-w[:, half:], w[:, :half]], axis=1)


def _rope_table(seq_len):
    inv = 1.0 / (ROPE_THETA ** (jnp.arange(0, QK_ROPE, 2, dtype=F32) / QK_ROPE))
    ang = jnp.arange(seq_len, dtype=F32)[:, None] * inv[None, :]
    cos, sin = jnp.cos(ang), jnp.sin(ang)
    ones = jnp.ones((seq_len, QK_NOPE), F32)
    pad = jnp.zeros((seq_len, LANES - QK_NOPE - QK_ROPE), F32)
    cos_t = jnp.concatenate([ones, cos, cos, pad], axis=1)
    sin_t = jnp.concatenate([0 * ones, sin, sin, pad], axis=1)
    return jnp.concatenate([cos_t, sin_t], axis=1)


def kernel(x, norm_g, ffn_w_in, ffn_conv_w, ffn_conv_b, ffn_w_out, a_mu, a_w_rkv, a_w0, a_w1, a_w2, a_a0, a_a1, a_a2, a_g1, a_g2, a_k_k, a_k_a, a_r_k, a_lnx_w, a_lnx_b, a_w_o, kv_norm_g, kv_w_down, kv_a_norm_g, kv_w_up, q_w_down, q_norm_g, q_w_up, o_w):
    batch, seq_len, d = x.shape
    n_heads = d // HEAD
    n_a = a_mu.shape[0]
    depth = norm_g.shape[0]
    xf = x.reshape(batch * seq_len, d)

    head_of = jnp.arange(d) // HEAD
    e = _bf(head_of[:, None] == jnp.arange(LANES)[None, :])
    et = e.T
    pair_of = jnp.arange(PAIR_W) // HEAD
    bd = _bf(pair_of[:, None] == pair_of[None, :])
    rope_cs = _rope_table(seq_len)
    zeros_rope = jnp.zeros((1, LANES - QK_NOPE - QK_ROPE), F32)

    kq = vtq = None
    for layer in range(depth):
        gl = norm_g[layer].astype(F32)
        if layer < n_a:
            i = layer
            vec = _rows([gl[0]] + [a_mu[i, n] for n in range(6)]
                        + [a_w0[i], a_a0[i], a_k_k[i], a_k_a[i]], d)
            lora = LANES
            gate_lora = 2 * LANES
            r, k, v, a, b, g, lw = _rwkv_prep(
                xf, seq_len, vec, _bf(a_w_rkv[i]),
                _bf(_pad_cols(a_w1[i], lora)), _bf(_pad_rows(a_w2[i], lora)),
                _bf(_pad_cols(a_a1[i], lora)), _bf(_pad_rows(a_a2[i], lora)),
                _bf(_pad_cols(a_g1[i], gate_lora)), _bf(_pad_rows(a_g2[i], gate_lora)), e, et)
            svec = _rows([a_lnx_w[i], a_lnx_b[i], a_r_k[i]], d)
            mix = _rwkv_scan(r, k, v, a, b, g, lw, svec, bd, batch, seq_len)
            w_out_proj = _bf(a_w_o[i])
        else:
            if layer == n_a:
                wd = kv_w_down[:, :KV_LORA]
                wr = kv_w_down[:, KV_LORA:]
                up = kv_w_up.reshape(KV_LORA, n_heads, 2 * HEAD)
                wk = _pad_cols(up[:, :, :QK_NOPE].reshape(KV_LORA * n_heads, QK_NOPE), LANES)
                wk = wk.reshape(KV_LORA, n_heads * LANES)
                wvt = _pad_cols(up[:, :, QK_NOPE:].reshape(KV_LORA * n_heads, HEAD), LANES)
                wvt = wvt.reshape(KV_LORA, n_heads * LANES).T
                lane = jnp.arange(n_heads * LANES) % LANES
                place = (lane[None, :] == (jnp.arange(LANES) + QK_NOPE)[:, None]) & (jnp.arange(LANES) < QK_ROPE)[:, None]
                kq, vtq = _kv_prep(
                    xf, batch, seq_len, kv_norm_g.reshape(1, d), kv_a_norm_g.reshape(1, KV_LORA),
                    _bf(wd), _bf(_pad_cols(wr, LANES)), _bf(_pad_cols(_rotate_half_cols(wr), LANES)),
                    jnp.concatenate([rope_cs[:, QK_NOPE:LANES], jnp.zeros((seq_len, QK_NOPE), F32),
                                     rope_cs[:, LANES + QK_NOPE:], jnp.zeros((seq_len, QK_NOPE), F32)], axis=1),
                    _bf(wk), _bf(wvt), _bf(place))
            j = layer - n_a
            qup = q_w_up[j].reshape(-1, n_heads, QK_NOPE + QK_ROPE)
            q_lora = qup.shape[0]
            wa = _pad_cols(qup.reshape(q_lora * n_heads, -1), LANES).reshape(q_lora, n_heads * LANES)
            rot = jnp.concatenate([jnp.zeros((q_lora, n_heads, QK_NOPE), F32),
                                   jnp.concatenate([-qup[:, :, QK_NOPE + QK_ROPE // 2:],
                                                    qup[:, :, QK_NOPE:QK_NOPE + QK_ROPE // 2]], axis=-1)], axis=-1)
            wb = _pad_cols(rot.reshape(q_lora * n_heads, -1), LANES).reshape(q_lora, n_heads * LANES)
            q = _q_prep(xf, seq_len, math.log2(math.e) / math.sqrt(QK_NOPE + QK_ROPE), gl[0:1], q_norm_g[j].reshape(1, -1),
                        _bf(q_w_down[j]), _bf(wa), _bf(wb), rope_cs)
            mix = _attention(q, kq, vtq, batch, seq_len)
            w_out_proj = _bf(o_w[j])
        d_ff = ffn_conv_b.shape[1]
        conv = _rows([ffn_conv_w[layer, 0], ffn_conv_w[layer, 1], ffn_conv_w[layer, 2], ffn_conv_b[layer]], d_ff)
        xf = _mix_ffn(xf, mix, seq_len, w_out_proj, _rows([gl[1], gl[2], gl[3]], d),
                      _bf(ffn_w_in[layer]), conv, _bf(ffn_w_out[layer]))
    return xf.reshape(batch, seq_len, d)
```

```python
import functools
import math

import jax
import jax.numpy as jnp
from jax import lax
from jax.experimental import pallas as pl
from jax.experimental.pallas import tpu as pltpu

F32 = jnp.float32
BF16 = jnp.bfloat16

HEAD = 64
LNX_EPS = 64e-5
NORM_EPS = 1e-6
QK_NOPE = 64
QK_ROPE = 32
KV_LORA = 256
ROPE_THETA = 10000.0
MASK_CHUNK = 64
NEG_INF = -1e30

LANES = 128
SUBLANES = 8
VMEM_LIMIT = 48 * 1024 * 1024

SCAN_CHUNK = 64
SCAN_BATCH = 4
ROW_TILE = 512
FFN_ROW_TILE = 512
FFN_COL_BLOCK = 256
FFN_LOOKAHEAD = 2
FFN_OUT_GROUP = 6
ATT_TILE = 1024
ATT_Q_SPLIT = 4


def _bf(x):
    return x.astype(BF16)


def _mm(a, b):
    return jnp.dot(a, b, preferred_element_type=F32)


def _mm_nt(a, b):
    return lax.dot_general(a, b, (((1,), (1,)), ((), ())), preferred_element_type=F32)


def _mm_tn(a, b):
    return lax.dot_general(a, b, (((0,), (0,)), ((), ())), preferred_element_type=F32)


def _rms(x, g):
    return x * lax.rsqrt(jnp.mean(x * x, axis=-1, keepdims=True) + NORM_EPS) * g


def _sigmoid(x):
    return 1.0 / (1.0 + jnp.exp2(x * -math.log2(math.e)))


def _params(*sem):
    return pltpu.CompilerParams(dimension_semantics=sem, vmem_limit_bytes=VMEM_LIMIT)


def _full(shape):
    nd = len(shape)
    return pl.BlockSpec(shape, lambda *_: (0,) * nd)


def _resident(shape):
    nd = len(shape)
    return pl.BlockSpec(shape, lambda *_: (0,) * nd, pipeline_mode=pl.Buffered(1))


def _rwkv_prep_kernel(seq_len, x_ref, xh_ref, vec_ref, wrkv_ref, w1_ref, w2_ref, a1_ref, a2_ref,
                      g1_ref, g2_ref, e_ref, et_ref,
                      r_ref, k_ref, v_ref, a_ref, b_ref, g_ref, lw_ref, h_ref, xx_ref):
    tm = x_ref.shape[0]
    gn = vec_ref[0:1, :]
    h = _rms(x_ref[...], gn)
    h_halo = _rms(xh_ref[SUBLANES - 1:SUBLANES, :], gn)
    at_start = (pl.program_id(0) * tm) % seq_len == 0
    h_halo = jnp.where(at_start, 0.0, h_halo)
    row = lax.broadcasted_iota(jnp.int32, h.shape, 0)
    h_prev = jnp.where(row == 0, h_halo, pltpu.roll(h, 1, 0))
    h_ref[...] = h
    xx_ref[...] = h_prev - h

    def mix(i):
        return _bf(h_ref[...] + xx_ref[...] * vec_ref[1 + i:2 + i, :])

    r = _mm(mix(0), wrkv_ref[0])
    k = _mm(mix(1), wrkv_ref[1])
    v = _mm(mix(2), wrkv_ref[2])
    zw = vec_ref[7:8, :] + _mm(_bf(jnp.tanh(_mm(mix(3), w1_ref[...]))), w2_ref[...])
    lw = -math.exp(-0.5) * _sigmoid(zw)
    a = _sigmoid(vec_ref[8:9, :] + _mm(_bf(_mm(mix(4), a1_ref[...])), a2_ref[...]))
    g = _mm(_bf(_sigmoid(_mm(mix(5), g1_ref[...]))), g2_ref[...])
    kk = k * vec_ref[9:10, :]
    ss = _mm(_bf(_mm(_bf(kk * kk), e_ref[...])), et_ref[...])
    kk = kk * lax.rsqrt(jnp.maximum(ss, 1e-24))
    k = k * (1.0 + (a - 1.0) * vec_ref[10:11, :])
    r_ref[...] = _bf(r)
    k_ref[...] = _bf(k)
    v_ref[...] = _bf(v)
    a_ref[...] = _bf(-kk)
    b_ref[...] = _bf(kk * a)
    g_ref[...] = _bf(g)
    lw_ref[...] = lw


def _rwkv_prep(x, seq_len, vec, wrkv, w1, w2, a1, a2, g1, g2, e, et):
    m, d = x.shape
    tm = ROW_TILE
    halo = tm // SUBLANES
    row_spec = pl.BlockSpec((tm, d), lambda i: (i, 0))
    out_bf = jax.ShapeDtypeStruct((m, d), BF16)
    return pl.pallas_call(
        functools.partial(_rwkv_prep_kernel, seq_len),
        grid=(m // tm,),
        in_specs=[row_spec,
                  pl.BlockSpec((SUBLANES, d), lambda i: (jnp.maximum(i * halo - 1, 0), 0)),
                  _full(vec.shape), _full(wrkv.shape), _full(w1.shape), _full(w2.shape),
                  _full(a1.shape), _full(a2.shape), _full(g1.shape), _full(g2.shape),
                  _full(e.shape), _full(et.shape)],
        out_specs=[row_spec] * 7,
        out_shape=[out_bf] * 6 + [jax.ShapeDtypeStruct((m, d), F32)],
        scratch_shapes=[pltpu.VMEM((tm, d), F32), pltpu.VMEM((tm, d), F32)],
        compiler_params=_params("parallel"),
        name="rwkv_prep",
    )(x, x, vec, wrkv, w1, w2, a1, a2, g1, g2, e, et)


PAIR_W = 2 * HEAD


def _unit_lower_inverse(nms, row, col):
    diff = row ^ col
    eye = jnp.where(row == col, 1.0, 0.0)
    n8 = [jnp.where((diff >> 3) == 0, nm, 0.0) for nm in nms]
    n8b = [_bf(n) for n in n8]
    n8_2 = [_mm(n, n) for n in n8b]
    n8_2b = [_bf(n) for n in n8_2]
    size = nms[0].shape[0]
    n43 = [_mm(jnp.concatenate([n2, n], axis=0), n2) for n, n2 in zip(n8b, n8_2b)]
    xs = [eye + n + n2 + m[size:] for n, n2, m in zip(n8, n8_2, n43)]
    xs = [x + _mm(_bf(x), _bf(m[:size])) for x, m in zip(xs, n43)]
    for shift in (3, 4, 5):
        s = 1 << shift
        starts = range(0, size, 2 * s)

        def lower(m):
            return jnp.concatenate([m[b + s:b + 2 * s] for b in starts], axis=0)

        offs = [_bf(jnp.where((diff >> shift) == 1, nm, 0.0)) for nm in nms]
        xbs = [_bf(x) for x in xs]
        ts = [_bf(_mm(_bf(lower(x)), off)) for x, off in zip(xs, offs)]
        upd = [_mm(t, xb) for t, xb in zip(ts, xbs)]
        xs = [jnp.concatenate([piece for i, b in enumerate(starts)
                               for piece in (x[b:b + s], x[b + s:b + 2 * s] + u[i * s:(i + 1) * s])], axis=0)
              for x, u in zip(xs, upd)]
    return xs


def _rwkv_scan_kernel(r_ref, k_ref, v_ref, a_ref, b_ref, g_ref, lw_ref, vec_ref, bd_ref,
                      o_ref, s_ref):
    nb, chunk, d = lw_ref.shape

    @pl.when(pl.program_id(1) == 0)
    def _():
        s_ref[...] = jnp.zeros_like(s_ref)

    trow = lax.broadcasted_iota(jnp.int32, (chunk, chunk), 0)
    tcol = lax.broadcasted_iota(jnp.int32, (chunk, chunk), 1)
    tri = _bf(jnp.where(trow >= tcol, 1.0, 0.0))

    def decayed(j):
        lw = lw_ref[j]
        lw_hi = _bf(lw)
        lw_lo = _bf(lw - lw_hi.astype(F32))
        c = _mm(tri, lw_hi) + _mm(tri, lw_lo)
        c_last = c[chunk - 1:chunk, :]
        r = r_ref[j].astype(F32)
        k = k_ref[j].astype(F32)
        a = a_ref[j].astype(F32)
        b = b_ref[j].astype(F32)
        e_inv = jnp.exp(-c)
        e_rem = jnp.exp(c_last - c)
        at = a * jnp.exp(c - lw)
        rt = r * jnp.exp(c)
        return dict(r=r, k=k, at=at, rt=rt, bt=b * e_inv, kt=k * e_inv, bh=_bf(b * e_rem), kh=_bf(k * e_rem),
                    g_last=jnp.exp(c_last), atb=_bf(at), rtb=_bf(rt))

    rows = [decayed(j) for j in range(nb)]

    row = lax.broadcasted_iota(jnp.int32, (PAIR_W, PAIR_W), 0)
    col = lax.broadcasted_iota(jnp.int32, (PAIR_W, PAIR_W), 1)
    strict = (row & (HEAD - 1)) > (col & (HEAD - 1))
    incl = (row & (HEAD - 1)) >= (col & (HEAD - 1))
    same_head = (row >> 6) == (col >> 6)
    first_half = lax.broadcasted_iota(jnp.int32, (chunk, PAIR_W), 1) < HEAD
    bd = bd_ref[...]

    def stack(x):
        return jnp.concatenate([_bf(jnp.where(first_half, x, 0.0)), _bf(jnp.where(first_half, 0.0, x))], axis=0)

    n_pairs = d // PAIR_W
    items = [(j, slice(i * PAIR_W, (i + 1) * PAIR_W)) for j in range(nb) for i in range(n_pairs)]
    n_items = len(items)

    def op(name):
        return [rows[j][name][:, sl] for j, sl in items]

    lhs = [jnp.concatenate([stack(x), stack(y_)], axis=0) for x, y_ in zip(op("at"), op("rt"))]
    rhs = [jnp.concatenate([stack(x), stack(y_)], axis=0) for x, y_ in zip(op("bt"), op("kt"))]
    p = [_mm_nt(l, r_) for l, r_ in zip(lhs, rhs)]
    n_ab = [jnp.where(strict, x[:PAIR_W, :PAIR_W], 0.0) for x in p]
    a_ak = [_bf(jnp.where(strict, x[:PAIR_W, PAIR_W:], 0.0)) for x in p]
    aa = [jnp.concatenate([_bf(jnp.where(incl, x[PAIR_W:, :PAIR_W], 0.0)),
                           _bf(jnp.where(incl, x[PAIR_W:, PAIR_W:], 0.0))], axis=1) for x in p]
    tinv = [_bf(x) for x in _unit_lower_inverse(n_ab, row, col)]
    tt = [jnp.concatenate([t, _bf(_mm(t, ak))], axis=1) for t, ak in zip(tinv, a_ak)]

    s0 = [s_ref[i] for i in range(n_items)]
    vp = [v_ref[j, :, sl] for j, sl in items]
    qq = [_mm_nt(jnp.concatenate([x, y_], axis=0), _bf(s)) for x, y_, s in zip(op("atb"), op("rtb"), s0)]
    qa = [_bf(x[:chunk]) for x in qq]
    qr = [x[chunk:] for x in qq]
    u_st = [_mm(tt[i], jnp.concatenate([qa[i], qa[i], vp[i], vp[i]], axis=0)) for i in range(n_items)]
    y_st = [jnp.concatenate([qr[i], qr[i]], axis=0)
            + _mm(aa[i], jnp.concatenate([_bf(u_st[i]), vp[i], vp[i]], axis=0)) for i in range(n_items)]
    u = [jnp.where(first_half, x[:chunk], x[chunk:]) for x in u_st]
    y = [jnp.where(first_half, x[:chunk], x[chunk:]) for x in y_st]
    ds = [_mm_tn(jnp.concatenate([_bf(u[i]), vp[i]], axis=0), jnp.concatenate([x, y_], axis=0))
          for i, (x, y_) in enumerate(zip(op("bh"), op("kh")))]
    for i, g_last in enumerate(op("g_last")):
        s_ref[i] = s0[i] * g_last + jnp.where(same_head, ds[i], 0.0)

    def head_sums(xs):
        tot = _mm(jnp.concatenate([_bf(x) for x in xs], axis=0), bd)
        return [tot[i * chunk:(i + 1) * chunk] for i in range(n_items)]

    mean = [x * (1.0 / HEAD) for x in head_sums(y)]
    dev = [x - m_ for x, m_ in zip(y, mean)]
    var = [x * (1.0 / HEAD) for x in head_sums([x * x for x in dev])]
    bonus = head_sums([r_ * k_ * vec_ref[2:3, sl] for r_, k_, (_, sl) in zip(op("r"), op("k"), items)])
    for i, (j, sl) in enumerate(items):
        yn = dev[i] * lax.rsqrt(var[i] + LNX_EPS) * vec_ref[0:1, sl] + vec_ref[1:2, sl]
        o_ref[j, :, sl] = _bf((yn + bonus[i] * vp[i].astype(F32)) * g_ref[j, :, sl].astype(F32))


def _rwkv_scan(r, k, v, a, b, g, lw, vec, bd, batch, seq_len):
    m, d = lw.shape
    nb = SCAN_BATCH
    blk = pl.BlockSpec((nb, SCAN_CHUNK, d), lambda bi, ci: (bi, ci, 0))
    seq = [x.reshape(batch, seq_len, d) for x in (r, k, v, a, b, g, lw)]
    out = pl.pallas_call(
        _rwkv_scan_kernel,
        grid=(batch // nb, seq_len // SCAN_CHUNK),
        in_specs=[blk] * 7 + [_full(vec.shape), _full(bd.shape)],
        out_specs=blk,
        out_shape=jax.ShapeDtypeStruct((batch, seq_len, d), BF16),
        scratch_shapes=[pltpu.VMEM((nb * d // PAIR_W, PAIR_W, PAIR_W), F32)],
        compiler_params=_params("parallel", "arbitrary"),
        name="rwkv_scan",
    )(*seq, vec, bd)
    return out.reshape(m, d)


def _gelu_times(x, y):
    k1 = -2.0 * math.sqrt(2.0 / math.pi) * math.log2(math.e)
    e = jnp.exp2(x * (k1 + (k1 * 0.044715) * (x * x)))
    return (x * y) / (1.0 + e)


def _mix_ffn_kernel(tiles_per_seq, x_ref, o_ref, wp_ref, vec_ref, win_ref, cw_ref, wo_ref,
                    out_ref, tail_ref):
    tm = x_ref.shape[0]
    fc = FFN_COL_BLOCK
    x1 = x_ref[...] + _rms(_mm(o_ref[...], wp_ref[...]), vec_ref[0:1, :])
    xn = _bf(_rms(x1, vec_ref[1:2, :]))

    @pl.when(pl.program_id(0) % tiles_per_seq == 0)
    def _():
        tail_ref[...] = jnp.zeros_like(tail_ref)

    row = lax.broadcasted_iota(jnp.int32, (tm, fc), 0)
    acc = jnp.zeros(x1.shape, F32)
    f = win_ref.shape[1] // 2
    n_blocks = f // fc

    def gate_up(c):
        return _mm(xn, win_ref[:, c * fc:(c + 1) * fc]), _mm(xn, win_ref[:, f + c * fc:f + (c + 1) * fc])

    ahead = [gate_up(c) for c in range(min(FFN_LOOKAHEAD, n_blocks))]
    hidden = []
    for c in range(n_blocks):
        cs = slice(c * fc, (c + 1) * fc)
        gate, up = ahead.pop(0)
        if c + FFN_LOOKAHEAD < n_blocks:
            ahead.append(gate_up(c + FFN_LOOKAHEAD))
        t1 = tail_ref[SUBLANES - 1:SUBLANES, cs]
        t2 = tail_ref[SUBLANES - 2:SUBLANES - 1, cs]
        prev1 = jnp.where(row == 0, t1, pltpu.roll(gate, 1, 0))
        prev2 = jnp.where(row == 0, t2, jnp.where(row == 1, t1, pltpu.roll(gate, 2, 0)))
        tail_ref[:, cs] = gate[tm - SUBLANES:, :]
        gc = cw_ref[3:4, cs] + prev2 * cw_ref[0:1, cs] + prev1 * cw_ref[1:2, cs] + gate * cw_ref[2:3, cs]
        hidden.append(_bf(_gelu_times(gc, up)))
        if len(hidden) == FFN_OUT_GROUP or c == n_blocks - 1:
            lo = (c + 1 - len(hidden)) * fc
            acc = acc + _mm(jnp.concatenate(hidden, axis=1), wo_ref[lo:(c + 1) * fc, :])
            hidden = []
    out_ref[...] = x1 + _rms(acc, vec_ref[2:3, :])


def _mix_ffn(x, o, seq_len, w_proj, vec, w_in, conv, w_out):
    m, d = x.shape
    tm = FFN_ROW_TILE
    f = w_out.shape[0]
    return pl.pallas_call(
        functools.partial(_mix_ffn_kernel, seq_len // tm),
        grid=(m // tm,),
        in_specs=[pl.BlockSpec((tm, d), lambda i: (i, 0)),
                  pl.BlockSpec((tm, o.shape[1]), lambda i: (i, 0)),
                  _resident(w_proj.shape), _resident(vec.shape), _resident(w_in.shape),
                  _resident(conv.shape), _resident(w_out.shape)],
        out_specs=pl.BlockSpec((tm, d), lambda i: (i, 0)),
        out_shape=jax.ShapeDtypeStruct((m, d), F32),
        scratch_shapes=[pltpu.VMEM((SUBLANES, f), F32)],
        compiler_params=_params("arbitrary"),
        name="mix_ffn",
    )(x, o, w_proj, vec, w_in, conv, w_out)


def _kv_prep_kernel(x_ref, g_ref, ga_ref, wd_ref, wra_ref, wrb_ref, cs_ref, wk_ref, wvt_ref, place_ref,
                    k_ref, vt_ref):
    xn = _bf(_rms(x_ref[...], g_ref[...]))
    ckv = _bf(_rms(_mm(xn, wd_ref[...]), ga_ref[...]))
    kr = _mm(xn, wra_ref[...]) * cs_ref[:, :LANES] + _mm(xn, wrb_ref[...]) * cs_ref[:, LANES:]
    k_ref[...] = _bf(_mm(ckv, wk_ref[...]) + _mm(_bf(kr), place_ref[...]))
    vt = _mm_nt(wvt_ref[...], ckv)
    ones_row = (lax.broadcasted_iota(jnp.int32, vt.shape, 0) & (LANES - 1)) == HEAD
    vt_ref[...] = _bf(jnp.where(ones_row, 1.0, vt))


def _kv_prep(x, batch, seq_len, g, ga, wd, wra, wrb, cs, wk, wvt, place):
    m, d = x.shape
    tm = ATT_TILE
    nt = seq_len // tm
    return pl.pallas_call(
        _kv_prep_kernel,
        grid=(m // tm,),
        in_specs=[pl.BlockSpec((tm, d), lambda i: (i, 0)),
                  _full(g.shape), _full(ga.shape), _full(wd.shape), _full(wra.shape), _full(wrb.shape),
                  pl.BlockSpec((tm, 2 * LANES), lambda i: (i % nt, 0)),
                  _full(wk.shape), _full(wvt.shape), _full(place.shape)],
        out_specs=[pl.BlockSpec((tm, wk.shape[1]), lambda i: (i, 0)),
                   pl.BlockSpec((None, None, wvt.shape[0], tm), lambda i: (i // nt, i % nt, 0, 0))],
        out_shape=[jax.ShapeDtypeStruct((m, wk.shape[1]), BF16),
                   jax.ShapeDtypeStruct((batch, nt, wvt.shape[0], tm), BF16)],
        compiler_params=_params("parallel"),
        name="mla_kv_prep",
    )(x, g, ga, wd, wra, wrb, cs, wk, wvt, place)


def _q_prep_kernel(scale, x_ref, g_ref, gq_ref, wd_ref, wa_ref, tab_ref, q_ref):
    xn = _bf(_rms(x_ref[...], g_ref[...]))
    cq = _bf(_rms(_mm(xn, wd_ref[...]), gq_ref[...]))
    qa = _mm(cq, wa_ref[...])
    tab = tab_ref[...] * scale
    for h in range(qa.shape[1] // LANES):
        sl = slice(h * LANES, (h + 1) * LANES)
        q_ref[:, sl] = _bf(qa[:, sl] * tab)


def _q_prep(x, seq_len, scale, g, gq, wd, wa, tab):
    m, d = x.shape
    tm = ROW_TILE
    nt = seq_len // tm
    return pl.pallas_call(
        functools.partial(_q_prep_kernel, scale),
        grid=(m // tm,),
        in_specs=[pl.BlockSpec((tm, d), lambda i: (i, 0)),
                  _full(g.shape), _full(gq.shape), _full(wd.shape), _full(wa.shape),
                  pl.BlockSpec((tm, LANES), lambda i: (i % nt, 0))],
        out_specs=pl.BlockSpec((tm, wa.shape[1]), lambda i: (i, 0)),
        out_shape=jax.ShapeDtypeStruct((m, wa.shape[1]), BF16),
        compiler_params=_params("parallel"),
        name="mla_q_prep",
    )(x, g, gq, wd, wa, tab)


def _attn_kernel(q_ref, k_ref, vt_ref, o_ref):
    tq = q_ref.shape[0]
    tk = vt_ref.shape[-1]
    tc = tq // ATT_Q_SPLIT
    qi = pl.program_id(2)
    streams = [(h, c) for h in range(2) for c in range(ATT_Q_SPLIT)]
    hsl = [slice(h * LANES, (h + 1) * LANES) for h in range(2)]
    qs = [q_ref[c * tc:(c + 1) * tc, hsl[h]] for h, c in streams]

    def step(s, vt, m_i, acc):
        m_new = jnp.maximum(m_i, jnp.max(s, axis=0, keepdims=True))
        alpha = jnp.exp2(m_i - m_new)
        p = jnp.exp2(s - m_new)
        return m_new, alpha * acc + _mm(vt, _bf(p))

    def body(j, carry):
        off = pl.multiple_of(j * tk, tk)
        kt = [k_ref[pl.ds(off, tk), sl] for sl in hsl]
        ss = [_mm_nt(kt[h], q) for (h, _), q in zip(streams, qs)]
        return tuple(step(s, vt_ref[j, hsl[h], :], *mc) for (h, _), s, mc in zip(streams, ss, carry))

    init = tuple((jnp.full((1, tc), NEG_INF, F32), jnp.zeros((LANES, tc), F32)) for _ in streams)
    carry = lax.fori_loop(0, qi, body, init)

    off = pl.multiple_of(qi * tk, tk)
    outs = []
    kpos = lax.broadcasted_iota(jnp.int32, (tc, tc), 0) // MASK_CHUNK
    qpos = lax.broadcasted_iota(jnp.int32, (tc, tc), 1) // MASK_CHUNK
    visible = kpos <= qpos
    ss = [_mm_nt(k_ref[pl.ds(off, (c + 1) * tc), hsl[h]], q) for (h, c), q in zip(streams, qs)]
    for (h, c), s, mc in zip(streams, ss, carry):
        nk = (c + 1) * tc
        s_last = jnp.where(visible, s[nk - tc:], NEG_INF)
        s = s_last if c == 0 else jnp.concatenate([s[:nk - tc], s_last], axis=0)
        _, acc = step(s, vt_ref[qi, hsl[h], :nk], *mc)
        outs.append(acc[:HEAD] / acc[HEAD:HEAD + 1])
    o_t = jnp.concatenate([jnp.concatenate(outs[h * ATT_Q_SPLIT:(h + 1) * ATT_Q_SPLIT], axis=1)
                           for h in range(2)], axis=0)
    o_ref[...] = _bf(o_t.T)


def _attention(q, k, vt, batch, seq_len):
    m = q.shape[0]
    nkt, rows, tk = vt.shape[1:]
    n_pairs = rows // (2 * LANES)
    tq = ATT_TILE
    nq = seq_len // tq
    return pl.pallas_call(
        _attn_kernel,
        grid=(batch, n_pairs, nq),
        in_specs=[pl.BlockSpec((tq, 2 * LANES), lambda b, p, i: (b * nq + i, p)),
                  pl.BlockSpec((seq_len, 2 * LANES), lambda b, p, i: (b, p)),
                  pl.BlockSpec((None, nkt, 2 * LANES, tk), lambda b, p, i: (b, 0, p, 0))],
        out_specs=pl.BlockSpec((tq, PAIR_W), lambda b, p, i: (b * nq + i, p)),
        out_shape=jax.ShapeDtypeStruct((m, n_pairs * PAIR_W), BF16),
        compiler_params=_params("parallel", "parallel", "arbitrary"),
        name="mla_attention",
    )(q, k, vt)


def _pad_cols(w, n):
    return jnp.pad(w, ((0, 0), (0, n - w.shape[1])))


def _pad_rows(w, n):
    return jnp.pad(w, ((0, n - w.shape[0]), (0, 0)))


def _rows(vectors, d):
    rows = jnp.stack([v.reshape(d).astype(F32) for v in vectors])
    return _pad_rows(rows, -(-rows.shape[0] // SUBLANES) * SUBLANES)


def _rotate_half_cols(w):
    half = w.shape[1] // 2
    return jnp.concatenate([-w[:, half:], w[:, :half]], axis=1)


def _rope_tables(seq_len):
    inv = 1.0 / (ROPE_THETA ** (jnp.arange(0, QK_ROPE, 2, dtype=F32) / QK_ROPE))
    ang = jnp.arange(seq_len, dtype=F32)[:, None] * inv[None, :]
    cos, sin = jnp.cos(ang), jnp.sin(ang)
    pad = jnp.zeros((seq_len, LANES - QK_ROPE), F32)
    k_tab = jnp.concatenate([cos, cos, pad, sin, sin, pad], axis=1)
    q_tab = jnp.concatenate([jnp.ones((seq_len, QK_NOPE), F32), cos, cos, sin, sin], axis=1)
    return k_tab, q_tab


def kernel(x, norm_g, ffn_w_in, ffn_conv_w, ffn_conv_b, ffn_w_out, a_mu, a_w_rkv, a_w0, a_w1, a_w2, a_a0, a_a1, a_a2, a_g1, a_g2, a_k_k, a_k_a, a_r_k, a_lnx_w, a_lnx_b, a_w_o, kv_norm_g, kv_w_down, kv_a_norm_g, kv_w_up, q_w_down, q_norm_g, q_w_up, o_w):
    batch, seq_len, d = x.shape
    n_heads = d // HEAD
    n_a = a_mu.shape[0]
    depth = norm_g.shape[0]
    xf = x.reshape(batch * seq_len, d)

    head_of = jnp.arange(d) // HEAD
    e = _bf(head_of[:, None] == jnp.arange(LANES)[None, :])
    et = e.T
    pair_of = jnp.arange(PAIR_W) // HEAD
    bd = _bf(pair_of[:, None] == pair_of[None, :])
    k_tab, q_tab = _rope_tables(seq_len)

    kq = vtq = None
    for layer in range(depth):
        gl = norm_g[layer].astype(F32)
        if layer < n_a:
            i = layer
            vec = _rows([gl[0]] + [a_mu[i, n] for n in range(6)]
                        + [a_w0[i], a_a0[i], a_k_k[i], a_k_a[i]], d)
            lora = LANES
            gate_lora = 2 * LANES
            r, k, v, a, b, g, lw = _rwkv_prep(
                xf, seq_len, vec, _bf(a_w_rkv[i]),
                _bf(_pad_cols(a_w1[i], lora)), _bf(_pad_rows(a_w2[i], lora)),
                _bf(_pad_cols(a_a1[i], lora)), _bf(_pad_rows(a_a2[i], lora)),
                _bf(_pad_cols(a_g1[i], gate_lora)), _bf(_pad_rows(a_g2[i], gate_lora)), e, et)
            svec = _rows([a_lnx_w[i], a_lnx_b[i], a_r_k[i]], d)
            mix = _rwkv_scan(r, k, v, a, b, g, lw, svec, bd, batch, seq_len)
            w_out_proj = _bf(a_w_o[i])
        else:
            if layer == n_a:
                wd = kv_w_down[:, :KV_LORA]
                wr = kv_w_down[:, KV_LORA:]
                up = kv_w_up.reshape(KV_LORA, n_heads, 2 * HEAD)
                wk = _pad_cols(up[:, :, :QK_NOPE].reshape(KV_LORA * n_heads, QK_NOPE), LANES)
                wk = wk.reshape(KV_LORA, n_heads * LANES)
                wvt = _pad_cols(up[:, :, QK_NOPE:].reshape(KV_LORA * n_heads, HEAD), LANES)
                wvt = wvt.reshape(KV_LORA, n_heads * LANES).T
                lane = (jnp.arange(n_heads * LANES) % LANES)[None, :]
                src = jnp.arange(LANES)[:, None]
                place = ((lane == src + QK_NOPE) | (lane == src + QK_NOPE + QK_ROPE)) & (src < QK_ROPE)
                kq, vtq = _kv_prep(
                    xf, batch, seq_len, kv_norm_g.reshape(1, d), kv_a_norm_g.reshape(1, KV_LORA),
                    _bf(wd), _bf(_pad_cols(wr, LANES)), _bf(_pad_cols(_rotate_half_cols(wr), LANES)),
                    k_tab, _bf(wk), _bf(wvt), _bf(place))
            j = layer - n_a
            qup = q_w_up[j].reshape(-1, n_heads, QK_NOPE + QK_ROPE)
            q_lora = qup.shape[0]
            wa = jnp.concatenate([qup, -qup[:, :, QK_NOPE + QK_ROPE // 2:],
                                  qup[:, :, QK_NOPE:QK_NOPE + QK_ROPE // 2]], axis=-1).reshape(q_lora, n_heads * LANES)
            q = _q_prep(xf, seq_len, math.log2(math.e) / math.sqrt(QK_NOPE + QK_ROPE), gl[0:1], q_norm_g[j].reshape(1, -1),
                        _bf(q_w_down[j]), _bf(wa), q_tab)
            mix = _attention(q, kq, vtq, batch, seq_len)
            w_out_proj = _bf(o_w[j])
        d_ff = ffn_conv_b.shape[1]
        conv = _rows([ffn_conv_w[layer, 0], ffn_conv_w[layer, 1], ffn_conv_w[layer, 2], ffn_conv_b[layer]], d_ff)
        xf = _mix_ffn(xf, mix, seq_len, w_out_proj, _rows([gl[1], gl[2], gl[3]], d),
                      _bf(ffn_w_in[layer]), conv, _bf(ffn_w_out[layer]))
    return xf.reshape(batch, seq_len, d)
```

```python
import functools
import math

import jax
import jax.numpy as jnp
from jax import lax
from jax.experimental import pallas as pl
from jax.experimental.pallas import tpu as pltpu

F32 = jnp.float32
BF16 = jnp.bfloat16

HEAD = 64
LNX_EPS = 64e-5
NORM_EPS = 1e-6
QK_NOPE = 64
QK_ROPE = 32
KV_LORA = 256
ROPE_THETA = 10000.0
MASK_CHUNK = 64
NEG_INF = -1e30

LANES = 128
SUBLANES = 8
VMEM_LIMIT = 48 * 1024 * 1024

SCAN_CHUNK = 64
SCAN_BATCH = 4
ROW_TILE = 512
FFN_ROW_TILE = 512
FFN_COL_BLOCK = 256
FFN_LOOKAHEAD = 2
FFN_OUT_GROUP = 6
ATT_TILE = 1024
ATT_Q_SPLIT = 4


def _bf(x):
    return x.astype(BF16)


def _mm(a, b):
    return jnp.dot(a, b, preferred_element_type=F32)


def _mm_nt(a, b):
    return lax.dot_general(a, b, (((1,), (1,)), ((), ())), preferred_element_type=F32)


def _mm_tn(a, b):
    return lax.dot_general(a, b, (((0,), (0,)), ((), ())), preferred_element_type=F32)


def _rms(x, g):
    return x * lax.rsqrt(jnp.mean(x * x, axis=-1, keepdims=True) + NORM_EPS) * g


def _sigmoid(x):
    return 1.0 / (1.0 + jnp.exp2(x * -math.log2(math.e)))


def _params(*sem):
    return pltpu.CompilerParams(dimension_semantics=sem, vmem_limit_bytes=VMEM_LIMIT)


def _full(shape):
    nd = len(shape)
    return pl.BlockSpec(shape, lambda *_: (0,) * nd)


def _resident(shape):
    nd = len(shape)
    return pl.BlockSpec(shape, lambda *_: (0,) * nd, pipeline_mode=pl.Buffered(1))


def _rwkv_prep_kernel(seq_len, x_ref, xh_ref, vec_ref, wrkv_ref, w1_ref, w2_ref, a1_ref, a2_ref,
                      g1_ref, g2_ref, e_ref, et_ref,
                      r_ref, k_ref, v_ref, a_ref, b_ref, g_ref, lw_ref, h_ref, xx_ref):
    tm = x_ref.shape[0]
    gn = vec_ref[0:1, :]
    h = _rms(x_ref[...], gn)
    h_halo = _rms(xh_ref[SUBLANES - 1:SUBLANES, :], gn)
    at_start = (pl.program_id(0) * tm) % seq_len == 0
    h_halo = jnp.where(at_start, 0.0, h_halo)
    row = lax.broadcasted_iota(jnp.int32, h.shape, 0)
    h_prev = jnp.where(row == 0, h_halo, pltpu.roll(h, 1, 0))
    h_ref[...] = h
    xx_ref[...] = h_prev - h

    def mix(i):
        return _bf(h_ref[...] + xx_ref[...] * vec_ref[1 + i:2 + i, :])

    r = _mm(mix(0), wrkv_ref[0])
    k = _mm(mix(1), wrkv_ref[1])
    v = _mm(mix(2), wrkv_ref[2])
    zw = vec_ref[7:8, :] + _mm(_bf(jnp.tanh(_mm(mix(3), w1_ref[...]))), w2_ref[...])
    lw = -math.exp(-0.5) * _sigmoid(zw)
    a = _sigmoid(vec_ref[8:9, :] + _mm(_bf(_mm(mix(4), a1_ref[...])), a2_ref[...]))
    g = _mm(_bf(_sigmoid(_mm(mix(5), g1_ref[...]))), g2_ref[...])
    kk = k * vec_ref[9:10, :]
    ss = _mm(_bf(_mm(_bf(kk * kk), e_ref[...])), et_ref[...])
    kk = kk * lax.rsqrt(jnp.maximum(ss, 1e-24))
    k = k * (1.0 + (a - 1.0) * vec_ref[10:11, :])
    r_ref[...] = _bf(r)
    k_ref[...] = _bf(k)
    v_ref[...] = _bf(v)
    a_ref[...] = _bf(-kk)
    b_ref[...] = _bf(kk * a)
    g_ref[...] = _bf(g)
    lw_ref[...] = lw


def _rwkv_prep(x, seq_len, vec, wrkv, w1, w2, a1, a2, g1, g2, e, et):
    m, d = x.shape
    tm = ROW_TILE
    halo = tm // SUBLANES
    row_spec = pl.BlockSpec((tm, d), lambda i: (i, 0))
    out_bf = jax.ShapeDtypeStruct((m, d), BF16)
    return pl.pallas_call(
        functools.partial(_rwkv_prep_kernel, seq_len),
        grid=(m // tm,),
        in_specs=[row_spec,
                  pl.BlockSpec((SUBLANES, d), lambda i: (jnp.maximum(i * halo - 1, 0), 0)),
                  _full(vec.shape), _full(wrkv.shape), _full(w1.shape), _full(w2.shape),
                  _full(a1.shape), _full(a2.shape), _full(g1.shape), _full(g2.shape),
                  _full(e.shape), _full(et.shape)],
        out_specs=[row_spec] * 7,
        out_shape=[out_bf] * 6 + [jax.ShapeDtypeStruct((m, d), F32)],
        scratch_shapes=[pltpu.VMEM((tm, d), F32), pltpu.VMEM((tm, d), F32)],
        compiler_params=_params("parallel"),
        name="rwkv_prep",
    )(x, x, vec, wrkv, w1, w2, a1, a2, g1, g2, e, et)


PAIR_W = 2 * HEAD


def _unit_lower_inverse(nms, row, col):
    diff = row ^ col
    eye = jnp.where(row == col, 1.0, 0.0)
    n8 = [jnp.where((diff >> 3) == 0, nm, 0.0) for nm in nms]
    n8b = [_bf(n) for n in n8]
    n8_2 = [_mm(n, n) for n in n8b]
    n8_2b = [_bf(n) for n in n8_2]
    size = nms[0].shape[0]
    n43 = [_mm(jnp.concatenate([n2, n], axis=0), n2) for n, n2 in zip(n8b, n8_2b)]
    xs = [eye + n + n2 + m[size:] for n, n2, m in zip(n8, n8_2, n43)]
    xs = [x + _mm(_bf(x), _bf(m[:size])) for x, m in zip(xs, n43)]
    for shift in (3, 4, 5):
        s = 1 << shift
        starts = range(0, size, 2 * s)

        def lower(m):
            return jnp.concatenate([m[b + s:b + 2 * s] for b in starts], axis=0)

        offs = [_bf(jnp.where((diff >> shift) == 1, nm, 0.0)) for nm in nms]
        xbs = [_bf(x) for x in xs]
        ts = [_bf(_mm(_bf(lower(x)), off)) for x, off in zip(xs, offs)]
        upd = [_mm(t, xb) for t, xb in zip(ts, xbs)]
        xs = [jnp.concatenate([piece for i, b in enumerate(starts)
                               for piece in (x[b:b + s], x[b + s:b + 2 * s] + u[i * s:(i + 1) * s])], axis=0)
              for x, u in zip(xs, upd)]
    return xs


def _rwkv_scan_kernel(r_ref, k_ref, v_ref, a_ref, b_ref, g_ref, lw_ref, vec_ref, bd_ref,
                      o_ref, s_ref):
    nb, chunk, d = lw_ref.shape

    @pl.when(pl.program_id(1) == 0)
    def _():
        s_ref[...] = jnp.zeros_like(s_ref)

    trow = lax.broadcasted_iota(jnp.int32, (chunk, chunk), 0)
    tcol = lax.broadcasted_iota(jnp.int32, (chunk, chunk), 1)
    tri = _bf(jnp.where(trow >= tcol, 1.0, 0.0))

    def decayed(j):
        lw = lw_ref[j]
        lw_hi = _bf(lw)
        lw_lo = _bf(lw - lw_hi.astype(F32))
        c = _mm(tri, lw_hi) + _mm(tri, lw_lo)
        c_last = c[chunk - 1:chunk, :]
        r = r_ref[j].astype(F32)
        k = k_ref[j].astype(F32)
        a = a_ref[j].astype(F32)
        b = b_ref[j].astype(F32)
        e_inv = jnp.exp(-c)
        e_rem = jnp.exp(c_last - c)
        at = a * jnp.exp(c - lw)
        rt = r * jnp.exp(c)
        return dict(r=r, k=k, at=at, rt=rt, bt=b * e_inv, kt=k * e_inv, bh=_bf(b * e_rem), kh=_bf(k * e_rem),
                    g_last=jnp.exp(c_last), atb=_bf(at), rtb=_bf(rt))

    rows = [decayed(j) for j in range(nb)]

    row = lax.broadcasted_iota(jnp.int32, (PAIR_W, PAIR_W), 0)
    col = lax.broadcasted_iota(jnp.int32, (PAIR_W, PAIR_W), 1)
    strict = (row & (HEAD - 1)) > (col & (HEAD - 1))
    incl = (row & (HEAD - 1)) >= (col & (HEAD - 1))
    same_head = (row >> 6) == (col >> 6)
    first_half = lax.broadcasted_iota(jnp.int32, (chunk, PAIR_W), 1) < HEAD
    bd = bd_ref[...]

    def stack(x):
        return jnp.concatenate([_bf(jnp.where(first_half, x, 0.0)), _bf(jnp.where(first_half, 0.0, x))], axis=0)

    n_pairs = d // PAIR_W
    items = [(j, slice(i * PAIR_W, (i + 1) * PAIR_W)) for j in range(nb) for i in range(n_pairs)]
    n_items = len(items)

    def op(name):
        return [rows[j][name][:, sl] for j, sl in items]

    lhs = [jnp.concatenate([stack(x), stack(y_)], axis=0) for x, y_ in zip(op("at"), op("rt"))]
    rhs = [jnp.concatenate([stack(x), stack(y_)], axis=0) for x, y_ in zip(op("bt"), op("kt"))]
    p = [_mm_nt(l, r_) for l, r_ in zip(lhs, rhs)]
    n_ab = [jnp.where(strict, x[:PAIR_W, :PAIR_W], 0.0) for x in p]
    a_ak = [_bf(jnp.where(strict, x[:PAIR_W, PAIR_W:], 0.0)) for x in p]
    aa = [jnp.concatenate([_bf(jnp.where(incl, x[PAIR_W:, :PAIR_W], 0.0)),
                           _bf(jnp.where(incl, x[PAIR_W:, PAIR_W:], 0.0))], axis=1) for x in p]
    tinv = [_bf(x) for x in _unit_lower_inverse(n_ab, row, col)]
    tt = [jnp.concatenate([t, _bf(_mm(t, ak))], axis=1) for t, ak in zip(tinv, a_ak)]

    s0 = [s_ref[i] for i in range(n_items)]
    vp = [v_ref[j, :, sl] for j, sl in items]
    qq = [_mm_nt(jnp.concatenate([x, y_], axis=0), _bf(s)) for x, y_, s in zip(op("atb"), op("rtb"), s0)]
    qa = [_bf(x[:chunk]) for x in qq]
    qr = [x[chunk:] for x in qq]
    u_st = [_mm(tt[i], jnp.concatenate([qa[i], qa[i], vp[i], vp[i]], axis=0)) for i in range(n_items)]
    y_st = [jnp.concatenate([qr[i], qr[i]], axis=0)
            + _mm(aa[i], jnp.concatenate([_bf(u_st[i]), vp[i], vp[i]], axis=0)) for i in range(n_items)]
    u = [jnp.where(first_half, x[:chunk], x[chunk:]) for x in u_st]
    y = [jnp.where(first_half, x[:chunk], x[chunk:]) for x in y_st]
    ds = [_mm_tn(jnp.concatenate([_bf(u[i]), vp[i]], axis=0), jnp.concatenate([x, y_], axis=0))
          for i, (x, y_) in enumerate(zip(op("bh"), op("kh")))]
    for i, g_last in enumerate(op("g_last")):
        s_ref[i] = s0[i] * g_last + jnp.where(same_head, ds[i], 0.0)

    def head_sums(xs):
        tot = _mm(jnp.concatenate([_bf(x) for x in xs], axis=0), bd)
        return [tot[i * chunk:(i + 1) * chunk] for i in range(n_items)]

    mean = [x * (1.0 / HEAD) for x in head_sums(y)]
    dev = [x - m_ for x, m_ in zip(y, mean)]
    var = [x * (1.0 / HEAD) for x in head_sums([x * x for x in dev])]
    bonus = head_sums([r_ * k_ * vec_ref[2:3, sl] for r_, k_, (_, sl) in zip(op("r"), op("k"), items)])
    for i, (j, sl) in enumerate(items):
        yn = dev[i] * lax.rsqrt(var[i] + LNX_EPS) * vec_ref[0:1, sl] + vec_ref[1:2, sl]
        o_ref[j, :, sl] = _bf((yn + bonus[i] * vp[i].astype(F32)) * g_ref[j, :, sl].astype(F32))


def _rwkv_scan(r, k, v, a, b, g, lw, vec, bd, batch, seq_len):
    m, d = lw.shape
    nb = SCAN_BATCH
    blk = pl.BlockSpec((nb, SCAN_CHUNK, d), lambda bi, ci: (bi, ci, 0))
    seq = [x.reshape(batch, seq_len, d) for x in (r, k, v, a, b, g, lw)]
    out = pl.pallas_call(
        _rwkv_scan_kernel,
        grid=(batch // nb, seq_len // SCAN_CHUNK),
        in_specs=[blk] * 7 + [_full(vec.shape), _full(bd.shape)],
        out_specs=blk,
        out_shape=jax.ShapeDtypeStruct((batch, seq_len, d), BF16),
        scratch_shapes=[pltpu.VMEM((nb * d // PAIR_W, PAIR_W, PAIR_W), F32)],
        compiler_params=_params("parallel", "arbitrary"),
        name="rwkv_scan",
    )(*seq, vec, bd)
    return out.reshape(m, d)


def _gelu_times(x, y):
    k1 = -2.0 * math.sqrt(2.0 / math.pi) * math.log2(math.e)
    e = jnp.exp2(x * (k1 + (k1 * 0.044715) * (x * x)))
    return (x * y) / (1.0 + e)


def _mix_ffn_kernel(tiles_per_seq, x_ref, o_ref, wp_ref, vec_ref, win_ref, cw_ref, wo_ref,
                    out_ref, tail_ref):
    tm = x_ref.shape[0]
    fc = FFN_COL_BLOCK
    x1 = x_ref[...] + _rms(_mm(o_ref[...], wp_ref[...]), vec_ref[0:1, :])
    xn = _bf(_rms(x1, vec_ref[1:2, :]))

    @pl.when(pl.program_id(0) % tiles_per_seq == 0)
    def _():
        tail_ref[...] = jnp.zeros_like(tail_ref)

    row = lax.broadcasted_iota(jnp.int32, (tm, fc), 0)
    acc = jnp.zeros(x1.shape, F32)
    f = win_ref.shape[1] // 2
    n_blocks = f // fc

    def gate_up(c):
        return _mm(xn, win_ref[:, c * fc:(c + 1) * fc]), _mm(xn, win_ref[:, f + c * fc:f + (c + 1) * fc])

    ahead = [gate_up(c) for c in range(min(FFN_LOOKAHEAD, n_blocks))]
    hidden = []
    for c in range(n_blocks):
        cs = slice(c * fc, (c + 1) * fc)
        gate, up = ahead.pop(0)
        if c + FFN_LOOKAHEAD < n_blocks:
            ahead.append(gate_up(c + FFN_LOOKAHEAD))
        t1 = tail_ref[SUBLANES - 1:SUBLANES, cs]
        t2 = tail_ref[SUBLANES - 2:SUBLANES - 1, cs]
        prev1 = jnp.where(row == 0, t1, pltpu.roll(gate, 1, 0))
        prev2 = jnp.where(row == 0, t2, jnp.where(row == 1, t1, pltpu.roll(gate, 2, 0)))
        tail_ref[:, cs] = gate[tm - SUBLANES:, :]
        gc = cw_ref[3:4, cs] + prev2 * cw_ref[0:1, cs] + prev1 * cw_ref[1:2, cs] + gate * cw_ref[2:3, cs]
        hidden.append(_bf(_gelu_times(gc, up)))
        if len(hidden) == FFN_OUT_GROUP or c == n_blocks - 1:
            lo = (c + 1 - len(hidden)) * fc
            acc = acc + _mm(jnp.concatenate(hidden, axis=1), wo_ref[lo:(c + 1) * fc, :])
            hidden = []
    out_ref[...] = x1 + _rms(acc, vec_ref[2:3, :])


def _mix_ffn(x, o, seq_len, w_proj, vec, w_in, conv, w_out):
    m, d = x.shape
    tm = FFN_ROW_TILE
    f = w_out.shape[0]
    return pl.pallas_call(
        functools.partial(_mix_ffn_kernel, seq_len // tm),
        grid=(m // tm,),
        in_specs=[pl.BlockSpec((tm, d), lambda i: (i, 0)),
                  pl.BlockSpec((tm, o.shape[1]), lambda i: (i, 0)),
                  _resident(w_proj.shape), _resident(vec.shape), _resident(w_in.shape),
                  _resident(conv.shape), _resident(w_out.shape)],
        out_specs=pl.BlockSpec((tm, d), lambda i: (i, 0)),
        out_shape=jax.ShapeDtypeStruct((m, d), F32),
        scratch_shapes=[pltpu.VMEM((SUBLANES, f), F32)],
        compiler_params=_params("arbitrary"),
        name="mix_ffn",
    )(x, o, w_proj, vec, w_in, conv, w_out)


def _kv_prep_kernel(x_ref, g_ref, ga_ref, wd_ref, wr_ref, cs_ref, wk_ref, wvt_ref, k_ref, vt_ref):
    xn = _bf(_rms(x_ref[...], g_ref[...]))
    ckv = _bf(_rms(_mm(xn, wd_ref[...]), ga_ref[...]))
    kr2 = _mm(xn, wr_ref[...]) * cs_ref[...]
    kr = kr2[:, :LANES] + kr2[:, LANES:]
    kn = _mm(ckv, wk_ref[...])
    for h in range(kn.shape[1] // LANES):
        sl = slice(h * LANES, (h + 1) * LANES)
        k_ref[:, sl] = _bf(kn[:, sl] + kr)
    vt = _mm_nt(wvt_ref[...], ckv)
    ones_row = (lax.broadcasted_iota(jnp.int32, vt.shape, 0) & (LANES - 1)) == HEAD
    vt_ref[...] = _bf(jnp.where(ones_row, 1.0, vt))


def _kv_prep(x, batch, seq_len, g, ga, wd, wr, cs, wk, wvt):
    m, d = x.shape
    tm = ATT_TILE
    nt = seq_len // tm
    return pl.pallas_call(
        _kv_prep_kernel,
        grid=(m // tm,),
        in_specs=[pl.BlockSpec((tm, d), lambda i: (i, 0)),
                  _full(g.shape), _full(ga.shape), _full(wd.shape), _full(wr.shape),
                  pl.BlockSpec((tm, 2 * LANES), lambda i: (i % nt, 0)),
                  _full(wk.shape), _full(wvt.shape)],
        out_specs=[pl.BlockSpec((tm, wk.shape[1]), lambda i: (i, 0)),
                   pl.BlockSpec((None, None, wvt.shape[0], tm), lambda i: (i // nt, i % nt, 0, 0))],
        out_shape=[jax.ShapeDtypeStruct((m, wk.shape[1]), BF16),
                   jax.ShapeDtypeStruct((batch, nt, wvt.shape[0], tm), BF16)],
        compiler_params=_params("parallel"),
        name="mla_kv_prep",
    )(x, g, ga, wd, wr, cs, wk, wvt)


def _q_prep_kernel(scale, x_ref, g_ref, gq_ref, wd_ref, wa_ref, tab_ref, q_ref):
    xn = _bf(_rms(x_ref[...], g_ref[...]))
    cq = _bf(_rms(_mm(xn, wd_ref[...]), gq_ref[...]))
    qa = _mm(cq, wa_ref[...])
    tab = tab_ref[...] * scale
    for h in range(qa.shape[1] // LANES):
        sl = slice(h * LANES, (h + 1) * LANES)
        q_ref[:, sl] = _bf(qa[:, sl] * tab)


def _q_prep(x, seq_len, scale, g, gq, wd, wa, tab):
    m, d = x.shape
    tm = ROW_TILE
    nt = seq_len // tm
    return pl.pallas_call(
        functools.partial(_q_prep_kernel, scale),
        grid=(m // tm,),
        in_specs=[pl.BlockSpec((tm, d), lambda i: (i, 0)),
                  _full(g.shape), _full(gq.shape), _full(wd.shape), _full(wa.shape),
                  pl.BlockSpec((tm, LANES), lambda i: (i % nt, 0))],
        out_specs=pl.BlockSpec((tm, wa.shape[1]), lambda i: (i, 0)),
        out_shape=jax.ShapeDtypeStruct((m, wa.shape[1]), BF16),
        compiler_params=_params("parallel"),
        name="mla_q_prep",
    )(x, g, gq, wd, wa, tab)


def _attn_kernel(q_ref, k_ref, vt_ref, o_ref):
    tq = q_ref.shape[0]
    tk = vt_ref.shape[-1]
    tc = tq // ATT_Q_SPLIT
    qi = pl.program_id(2)
    streams = [(h, c) for h in range(2) for c in range(ATT_Q_SPLIT)]
    hsl = [slice(h * LANES, (h + 1) * LANES) for h in range(2)]
    qs = [q_ref[c * tc:(c + 1) * tc, hsl[h]] for h, c in streams]

    def step(s, vt, m_i, acc):
        m_new = jnp.maximum(m_i, jnp.max(s, axis=0, keepdims=True))
        alpha = jnp.exp2(m_i - m_new)
        p = jnp.exp2(s - m_new)
        return m_new, alpha * acc + _mm(vt, _bf(p))

    def body(j, carry):
        off = pl.multiple_of(j * tk, tk)
        kt = [k_ref[pl.ds(off, tk), sl] for sl in hsl]
        ss = [_mm_nt(kt[h], q) for (h, _), q in zip(streams, qs)]
        return tuple(step(s, vt_ref[j, hsl[h], :], *mc) for (h, _), s, mc in zip(streams, ss, carry))

    init = tuple((jnp.full((1, tc), NEG_INF, F32), jnp.zeros((LANES, tc), F32)) for _ in streams)
    carry = lax.fori_loop(0, qi, body, init)

    off = pl.multiple_of(qi * tk, tk)
    outs = []
    kpos = lax.broadcasted_iota(jnp.int32, (tc, tc), 0) // MASK_CHUNK
    qpos = lax.broadcasted_iota(jnp.int32, (tc, tc), 1) // MASK_CHUNK
    visible = kpos <= qpos
    ss = [_mm_nt(k_ref[pl.ds(off, (c + 1) * tc), hsl[h]], q) for (h, c), q in zip(streams, qs)]
    for (h, c), s, mc in zip(streams, ss, carry):
        nk = (c + 1) * tc
        s_last = jnp.where(visible, s[nk - tc:], NEG_INF)
        s = s_last if c == 0 else jnp.concatenate([s[:nk - tc], s_last], axis=0)
        _, acc = step(s, vt_ref[qi, hsl[h], :nk], *mc)
        outs.append(acc[:HEAD] / acc[HEAD:HEAD + 1])
    o_t = jnp.concatenate([jnp.concatenate(outs[h * ATT_Q_SPLIT:(h + 1) * ATT_Q_SPLIT], axis=1)
                           for h in range(2)], axis=0)
    o_ref[...] = _bf(o_t.T)


def _attention(q, k, vt, batch, seq_len):
    m = q.shape[0]
    nkt, rows, tk = vt.shape[1:]
    n_pairs = rows // (2 * LANES)
    tq = ATT_TILE
    nq = seq_len // tq
    return pl.pallas_call(
        _attn_kernel,
        grid=(batch, n_pairs, nq),
        in_specs=[pl.BlockSpec((tq, 2 * LANES), lambda b, p, i: (b * nq + i, p)),
                  pl.BlockSpec((seq_len, 2 * LANES), lambda b, p, i: (b, p)),
                  pl.BlockSpec((None, nkt, 2 * LANES, tk), lambda b, p, i: (b, 0, p, 0))],
        out_specs=pl.BlockSpec((tq, PAIR_W), lambda b, p, i: (b * nq + i, p)),
        out_shape=jax.ShapeDtypeStruct((m, n_pairs * PAIR_W), BF16),
        compiler_params=_params("parallel", "parallel", "arbitrary"),
        name="mla_attention",
    )(q, k, vt)


def _pad_cols(w, n):
    return jnp.pad(w, ((0, 0), (0, n - w.shape[1])))


def _pad_rows(w, n):
    return jnp.pad(w, ((0, n - w.shape[0]), (0, 0)))


def _rows(vectors, d):
    rows = jnp.stack([v.reshape(d).astype(F32) for v in vectors])
    return _pad_rows(rows, -(-rows.shape[0] // SUBLANES) * SUBLANES)


def _rotate_half_cols(w):
    half = w.shape[1] // 2
    return jnp.concatenate([-w[:, half:], w[:, :half]], axis=1)


def _rope_tables(seq_len):
    inv = 1.0 / (ROPE_THETA ** (jnp.arange(0, QK_ROPE, 2, dtype=F32) / QK_ROPE))
    ang = jnp.arange(seq_len, dtype=F32)[:, None] * inv[None, :]
    cos, sin = jnp.cos(ang), jnp.sin(ang)
    pad = jnp.zeros((seq_len, QK_NOPE), F32)
    k_tab = jnp.concatenate([pad, cos, cos, cos, cos, pad, sin, sin, sin, sin], axis=1)
    q_tab = jnp.concatenate([jnp.ones((seq_len, QK_NOPE), F32), cos, cos, sin, sin], axis=1)
    return k_tab, q_tab


def kernel(x, norm_g, ffn_w_in, ffn_conv_w, ffn_conv_b, ffn_w_out, a_mu, a_w_rkv, a_w0, a_w1, a_w2, a_a0, a_a1, a_a2, a_g1, a_g2, a_k_k, a_k_a, a_r_k, a_lnx_w, a_lnx_b, a_w_o, kv_norm_g, kv_w_down, kv_a_norm_g, kv_w_up, q_w_down, q_norm_g, q_w_up, o_w):
    batch, seq_len, d = x.shape
    n_heads = d // HEAD
    n_a = a_mu.shape[0]
    depth = norm_g.shape[0]
    xf = x.reshape(batch * seq_len, d)

    head_of = jnp.arange(d) // HEAD
    e = _bf(head_of[:, None] == jnp.arange(LANES)[None, :])
    et = e.T
    pair_of = jnp.arange(PAIR_W) // HEAD
    bd = _bf(pair_of[:, None] == pair_of[None, :])
    k_tab, q_tab = _rope_tables(seq_len)

    kq = vq = None
    for layer in range(depth):
        gl = norm_g[layer].astype(F32)
        if layer < n_a:
            i = layer
            vec = _rows([gl[0]] + [a_mu[i, n] for n in range(6)]
                        + [a_w0[i], a_a0[i], a_k_k[i], a_k_a[i]], d)
            lora = LANES
            gate_lora = 2 * LANES
            r, k, v, a, b, g, lw = _rwkv_prep(
                xf, seq_len, vec, _bf(a_w_rkv[i]),
                _bf(_pad_cols(a_w1[i], lora)), _bf(_pad_rows(a_w2[i], lora)),
                _bf(_pad_cols(a_a1[i], lora)), _bf(_pad_rows(a_a2[i], lora)),
                _bf(_pad_cols(a_g1[i], gate_lora)), _bf(_pad_rows(a_g2[i], gate_lora)), e, et)
            svec = _rows([a_lnx_w[i], a_lnx_b[i], a_r_k[i]], d)
            mix = _rwkv_scan(r, k, v, a, b, g, lw, svec, bd, batch, seq_len)
            w_out_proj = _bf(a_w_o[i])
        else:
            if layer == n_a:
                wd = kv_w_down[:, :KV_LORA]
                wr = kv_w_down[:, KV_LORA:]
                up = kv_w_up.reshape(KV_LORA, n_heads, 2 * HEAD)
                wk = _pad_cols(up[:, :, :QK_NOPE].reshape(KV_LORA * n_heads, QK_NOPE), LANES)
                wk = wk.reshape(KV_LORA, n_heads * LANES)
                wvt = _pad_cols(up[:, :, QK_NOPE:].reshape(KV_LORA * n_heads, HEAD), LANES)
                wvt = wvt.reshape(KV_LORA, n_heads * LANES).T
                nope_pad = jnp.zeros((d, QK_NOPE), F32)
                wr_rot = _rotate_half_cols(wr)
                wr2 = jnp.concatenate([nope_pad, wr, wr, nope_pad, wr_rot, wr_rot], axis=1)
                kq, vq = _kv_prep(
                    xf, batch, seq_len, kv_norm_g.reshape(1, d), kv_a_norm_g.reshape(1, KV_LORA),
                    _bf(wd), _bf(wr2), k_tab, _bf(wk), _bf(wvt))
            j = layer - n_a
            qup = q_w_up[j].reshape(-1, n_heads, QK_NOPE + QK_ROPE)
            q_lora = qup.shape[0]
            wa = jnp.concatenate([qup, -qup[:, :, QK_NOPE + QK_ROPE // 2:],
                                  qup[:, :, QK_NOPE:QK_NOPE + QK_ROPE // 2]], axis=-1).reshape(q_lora, n_heads * LANES)
            q = _q_prep(xf, seq_len, math.log2(math.e) / math.sqrt(QK_NOPE + QK_ROPE), gl[0:1], q_norm_g[j].reshape(1, -1),
                        _bf(q_w_down[j]), _bf(wa), q_tab)
            mix = _attention(q, kq, vq, batch, seq_len)
            w_out_proj = _bf(o_w[j])
        d_ff = ffn_conv_b.shape[1]
        conv = _rows([ffn_conv_w[layer, 0], ffn_conv_w[layer, 1], ffn_conv_w[layer, 2], ffn_conv_b[layer]], d_ff)
        xf = _mix_ffn(xf, mix, seq_len, w_out_proj, _rows([gl[1], gl[2], gl[3]], d),
                      _bf(ffn_w_in[layer]), conv, _bf(ffn_w_out[layer]))
    return xf.reshape(batch, seq_len, d)
```

```python
import functools
import math

import jax
import jax.numpy as jnp
from jax import lax
from jax.experimental import pallas as pl
from jax.experimental.pallas import tpu as pltpu

F32 = jnp.float32
BF16 = jnp.bfloat16

HEAD = 64
LNX_EPS = 64e-5
NORM_EPS = 1e-6
QK_NOPE = 64
QK_ROPE = 32
KV_LORA = 256
ROPE_THETA = 10000.0
MASK_CHUNK = 64
NEG_INF = -1e30

LANES = 128
SUBLANES = 8
VMEM_LIMIT = 48 * 1024 * 1024

SCAN_CHUNK = 64
SCAN_BATCH = 4
ROW_TILE = 512
FFN_ROW_TILE = 512
FFN_COL_BLOCK = 256
FFN_LOOKAHEAD = 2
FFN_OUT_GROUP = 6
ATT_TILE = 1024
ATT_Q_TILE = 2048
ATT_Q_BLOCK = 256


def _bf(x):
    return x.astype(BF16)


def _mm(a, b):
    return jnp.dot(a, b, preferred_element_type=F32)


def _mm_nt(a, b):
    return lax.dot_general(a, b, (((1,), (1,)), ((), ())), preferred_element_type=F32)


def _mm_tn(a, b):
    return lax.dot_general(a, b, (((0,), (0,)), ((), ())), preferred_element_type=F32)


def _rms(x, g):
    return x * lax.rsqrt(jnp.mean(x * x, axis=-1, keepdims=True) + NORM_EPS) * g


def _sigmoid(x):
    return 1.0 / (1.0 + jnp.exp2(x * -math.log2(math.e)))


def _params(*sem):
    return pltpu.CompilerParams(dimension_semantics=sem, vmem_limit_bytes=VMEM_LIMIT)


def _full(shape):
    nd = len(shape)
    return pl.BlockSpec(shape, lambda *_: (0,) * nd)


def _resident(shape):
    nd = len(shape)
    return pl.BlockSpec(shape, lambda *_: (0,) * nd, pipeline_mode=pl.Buffered(1))


def _rwkv_prep_kernel(seq_len, x_ref, xh_ref, vec_ref, wrkv_ref, w1_ref, w2_ref, a1_ref, a2_ref,
                      g1_ref, g2_ref, e_ref, et_ref,
                      r_ref, k_ref, v_ref, a_ref, b_ref, g_ref, lw_ref, h_ref, xx_ref):
    tm = x_ref.shape[0]
    gn = vec_ref[0:1, :]
    h = _rms(x_ref[...], gn)
    h_halo = _rms(xh_ref[SUBLANES - 1:SUBLANES, :], gn)
    at_start = (pl.program_id(0) * tm) % seq_len == 0
    h_halo = jnp.where(at_start, 0.0, h_halo)
    row = lax.broadcasted_iota(jnp.int32, h.shape, 0)
    h_prev = jnp.where(row == 0, h_halo, pltpu.roll(h, 1, 0))
    h_ref[...] = h
    xx_ref[...] = h_prev - h

    def mix(i):
        return _bf(h_ref[...] + xx_ref[...] * vec_ref[1 + i:2 + i, :])

    r = _mm(mix(0), wrkv_ref[0])
    k = _mm(mix(1), wrkv_ref[1])
    v = _mm(mix(2), wrkv_ref[2])
    zw = vec_ref[7:8, :] + _mm(_bf(jnp.tanh(_mm(mix(3), w1_ref[...]))), w2_ref[...])
    lw = -math.exp(-0.5) * _sigmoid(zw)
    a = _sigmoid(vec_ref[8:9, :] + _mm(_bf(_mm(mix(4), a1_ref[...])), a2_ref[...]))
    g = _mm(_bf(_sigmoid(_mm(mix(5), g1_ref[...]))), g2_ref[...])
    kk = k * vec_ref[9:10, :]
    ss = _mm(_bf(_mm(_bf(kk * kk), e_ref[...])), et_ref[...])
    kk = kk * lax.rsqrt(jnp.maximum(ss, 1e-24))
    k = k * (1.0 + (a - 1.0) * vec_ref[10:11, :])
    r_ref[...] = _bf(r)
    k_ref[...] = _bf(k)
    v_ref[...] = _bf(v)
    a_ref[...] = _bf(-kk)
    b_ref[...] = _bf(kk * a)
    g_ref[...] = _bf(g)
    lw_ref[...] = lw


def _rwkv_prep(x, seq_len, vec, wrkv, w1, w2, a1, a2, g1, g2, e, et):
    m, d = x.shape
    tm = ROW_TILE
    halo = tm // SUBLANES
    row_spec = pl.BlockSpec((tm, d), lambda i: (i, 0))
    out_bf = jax.ShapeDtypeStruct((m, d), BF16)
    return pl.pallas_call(
        functools.partial(_rwkv_prep_kernel, seq_len),
        grid=(m // tm,),
        in_specs=[row_spec,
                  pl.BlockSpec((SUBLANES, d), lambda i: (jnp.maximum(i * halo - 1, 0), 0)),
                  _full(vec.shape), _full(wrkv.shape), _full(w1.shape), _full(w2.shape),
                  _full(a1.shape), _full(a2.shape), _full(g1.shape), _full(g2.shape),
                  _full(e.shape), _full(et.shape)],
        out_specs=[row_spec] * 7,
        out_shape=[out_bf] * 6 + [jax.ShapeDtypeStruct((m, d), F32)],
        scratch_shapes=[pltpu.VMEM((tm, d), F32), pltpu.VMEM((tm, d), F32)],
        compiler_params=_params("parallel"),
        name="rwkv_prep",
    )(x, x, vec, wrkv, w1, w2, a1, a2, g1, g2, e, et)


PAIR_W = 2 * HEAD


def _unit_lower_inverse(nms, row, col):
    diff = row ^ col
    eye = jnp.where(row == col, 1.0, 0.0)
    n8 = [jnp.where((diff >> 3) == 0, nm, 0.0) for nm in nms]
    n8b = [_bf(n) for n in n8]
    n8_2 = [_mm(n, n) for n in n8b]
    n8_2b = [_bf(n) for n in n8_2]
    size = nms[0].shape[0]
    n43 = [_mm(jnp.concatenate([n2, n], axis=0), n2) for n, n2 in zip(n8b, n8_2b)]
    xs = [eye + n + n2 + m[size:] for n, n2, m in zip(n8, n8_2, n43)]
    xs = [x + _mm(_bf(x), _bf(m[:size])) for x, m in zip(xs, n43)]
    for shift in (3, 4, 5):
        s = 1 << shift
        starts = range(0, size, 2 * s)

        def lower(m):
            return jnp.concatenate([m[b + s:b + 2 * s] for b in starts], axis=0)

        offs = [_bf(jnp.where((diff >> shift) == 1, nm, 0.0)) for nm in nms]
        xbs = [_bf(x) for x in xs]
        ts = [_bf(_mm(_bf(lower(x)), off)) for x, off in zip(xs, offs)]
        upd = [_mm(t, xb) for t, xb in zip(ts, xbs)]
        xs = [jnp.concatenate([piece for i, b in enumerate(starts)
                               for piece in (x[b:b + s], x[b + s:b + 2 * s] + u[i * s:(i + 1) * s])], axis=0)
              for x, u in zip(xs, upd)]
    return xs


def _rwkv_scan_kernel(r_ref, k_ref, v_ref, a_ref, b_ref, g_ref, lw_ref, vec_ref, bd_ref,
                      o_ref, s_ref):
    nb, chunk, d = lw_ref.shape

    @pl.when(pl.program_id(1) == 0)
    def _():
        s_ref[...] = jnp.zeros_like(s_ref)

    trow = lax.broadcasted_iota(jnp.int32, (chunk, chunk), 0)
    tcol = lax.broadcasted_iota(jnp.int32, (chunk, chunk), 1)
    tri = _bf(jnp.where(trow >= tcol, 1.0, 0.0))

    def decayed(j):
        lw = lw_ref[j]
        lw_hi = _bf(lw)
        lw_lo = _bf(lw - lw_hi.astype(F32))
        c = _mm(tri, lw_hi) + _mm(tri, lw_lo)
        c_last = c[chunk - 1:chunk, :]
        r = r_ref[j].astype(F32)
        k = k_ref[j].astype(F32)
        a = a_ref[j].astype(F32)
        b = b_ref[j].astype(F32)
        e_inv = jnp.exp(-c)
        e_rem = jnp.exp(c_last - c)
        at = a * jnp.exp(c - lw)
        rt = r * jnp.exp(c)
        return dict(r=r, k=k, at=at, rt=rt, bt=b * e_inv, kt=k * e_inv, bh=_bf(b * e_rem), kh=_bf(k * e_rem),
                    g_last=jnp.exp(c_last), atb=_bf(at), rtb=_bf(rt))

    rows = [decayed(j) for j in range(nb)]

    row = lax.broadcasted_iota(jnp.int32, (PAIR_W, PAIR_W), 0)
    col = lax.broadcasted_iota(jnp.int32, (PAIR_W, PAIR_W), 1)
    strict = (row & (HEAD - 1)) > (col & (HEAD - 1))
    incl = (row & (HEAD - 1)) >= (col & (HEAD - 1))
    same_head = (row >> 6) == (col >> 6)
    first_half = lax.broadcasted_iota(jnp.int32, (chunk, PAIR_W), 1) < HEAD
    bd = bd_ref[...]

    def stack(x):
        return jnp.concatenate([_bf(jnp.where(first_half, x, 0.0)), _bf(jnp.where(first_half, 0.0, x))], axis=0)

    n_pairs = d // PAIR_W
    items = [(j, slice(i * PAIR_W, (i + 1) * PAIR_W)) for j in range(nb) for i in range(n_pairs)]
    n_items = len(items)

    def op(name):
        return [rows[j][name][:, sl] for j, sl in items]

    lhs = [jnp.concatenate([stack(x), stack(y_)], axis=0) for x, y_ in zip(op("at"), op("rt"))]
    rhs = [jnp.concatenate([stack(x), stack(y_)], axis=0) for x, y_ in zip(op("bt"), op("kt"))]
    p = [_mm_nt(l, r_) for l, r_ in zip(lhs, rhs)]
    n_ab = [jnp.where(strict, x[:PAIR_W, :PAIR_W], 0.0) for x in p]
    a_ak = [_bf(jnp.where(strict, x[:PAIR_W, PAIR_W:], 0.0)) for x in p]
    aa = [jnp.concatenate([_bf(jnp.where(incl, x[PAIR_W:, :PAIR_W], 0.0)),
                           _bf(jnp.where(incl, x[PAIR_W:, PAIR_W:], 0.0))], axis=1) for x in p]
    tinv = [_bf(x) for x in _unit_lower_inverse(n_ab, row, col)]
    tt = [jnp.concatenate([t, _bf(_mm(t, ak))], axis=1) for t, ak in zip(tinv, a_ak)]

    s0 = [s_ref[i] for i in range(n_items)]
    vp = [v_ref[j, :, sl] for j, sl in items]
    qq = [_mm_nt(jnp.concatenate([x, y_], axis=0), _bf(s)) for x, y_, s in zip(op("atb"), op("rtb"), s0)]
    qa = [_bf(x[:chunk]) for x in qq]
    qr = [x[chunk:] for x in qq]
    u_st = [_mm(tt[i], jnp.concatenate([qa[i], qa[i], vp[i], vp[i]], axis=0)) for i in range(n_items)]
    y_st = [jnp.concatenate([qr[i], qr[i]], axis=0)
            + _mm(aa[i], jnp.concatenate([_bf(u_st[i]), vp[i], vp[i]], axis=0)) for i in range(n_items)]
    u = [jnp.where(first_half, x[:chunk], x[chunk:]) for x in u_st]
    y = [jnp.where(first_half, x[:chunk], x[chunk:]) for x in y_st]
    ds = [_mm_tn(jnp.concatenate([_bf(u[i]), vp[i]], axis=0), jnp.concatenate([x, y_], axis=0))
          for i, (x, y_) in enumerate(zip(op("bh"), op("kh")))]
    for i, g_last in enumerate(op("g_last")):
        s_ref[i] = s0[i] * g_last + jnp.where(same_head, ds[i], 0.0)

    def head_sums(xs):
        tot = _mm(jnp.concatenate([_bf(x) for x in xs], axis=0), bd)
        return [tot[i * chunk:(i + 1) * chunk] for i in range(n_items)]

    mean = [x * (1.0 / HEAD) for x in head_sums(y)]
    dev = [x - m_ for x, m_ in zip(y, mean)]
    var = [x * (1.0 / HEAD) for x in head_sums([x * x for x in dev])]
    bonus = head_sums([r_ * k_ * vec_ref[2:3, sl] for r_, k_, (_, sl) in zip(op("r"), op("k"), items)])
    for i, (j, sl) in enumerate(items):
        yn = dev[i] * lax.rsqrt(var[i] + LNX_EPS) * vec_ref[0:1, sl] + vec_ref[1:2, sl]
        o_ref[j, :, sl] = _bf((yn + bonus[i] * vp[i].astype(F32)) * g_ref[j, :, sl].astype(F32))


def _rwkv_scan(r, k, v, a, b, g, lw, vec, bd, batch, seq_len):
    m, d = lw.shape
    nb = SCAN_BATCH
    blk = pl.BlockSpec((nb, SCAN_CHUNK, d), lambda bi, ci: (bi, ci, 0))
    seq = [x.reshape(batch, seq_len, d) for x in (r, k, v, a, b, g, lw)]
    out = pl.pallas_call(
        _rwkv_scan_kernel,
        grid=(batch // nb, seq_len // SCAN_CHUNK),
        in_specs=[blk] * 7 + [_full(vec.shape), _full(bd.shape)],
        out_specs=blk,
        out_shape=jax.ShapeDtypeStruct((batch, seq_len, d), BF16),
        scratch_shapes=[pltpu.VMEM((nb * d // PAIR_W, PAIR_W, PAIR_W), F32)],
        compiler_params=_params("parallel", "arbitrary"),
        name="rwkv_scan",
    )(*seq, vec, bd)
    return out.reshape(m, d)


def _gelu_times(x, y):
    k1 = -2.0 * math.sqrt(2.0 / math.pi) * math.log2(math.e)
    e = jnp.exp2(x * (k1 + (k1 * 0.044715) * (x * x)))
    return (x * y) / (1.0 + e)


def _mix_ffn_kernel(tiles_per_seq, x_ref, o_ref, wp_ref, vec_ref, win_ref, cw_ref, wo_ref,
                    out_ref, tail_ref):
    tm = x_ref.shape[0]
    fc = FFN_COL_BLOCK
    x1 = x_ref[...] + _rms(_mm(o_ref[...], wp_ref[...]), vec_ref[0:1, :])
    xn = _bf(_rms(x1, vec_ref[1:2, :]))

    @pl.when(pl.program_id(0) % tiles_per_seq == 0)
    def _():
        tail_ref[...] = jnp.zeros_like(tail_ref)

    row = lax.broadcasted_iota(jnp.int32, (tm, fc), 0)
    acc = jnp.zeros(x1.shape, F32)
    f = win_ref.shape[1] // 2
    n_blocks = f // fc

    def gate_up(c):
        return _mm(xn, win_ref[:, c * fc:(c + 1) * fc]), _mm(xn, win_ref[:, f + c * fc:f + (c + 1) * fc])

    ahead = [gate_up(c) for c in range(min(FFN_LOOKAHEAD, n_blocks))]
    hidden = []
    for c in range(n_blocks):
        cs = slice(c * fc, (c + 1) * fc)
        gate, up = ahead.pop(0)
        if c + FFN_LOOKAHEAD < n_blocks:
            ahead.append(gate_up(c + FFN_LOOKAHEAD))
        t1 = tail_ref[SUBLANES - 1:SUBLANES, cs]
        t2 = tail_ref[SUBLANES - 2:SUBLANES - 1, cs]
        prev1 = jnp.where(row == 0, t1, pltpu.roll(gate, 1, 0))
        prev2 = jnp.where(row == 0, t2, jnp.where(row == 1, t1, pltpu.roll(gate, 2, 0)))
        tail_ref[:, cs] = gate[tm - SUBLANES:, :]
        gc = cw_ref[3:4, cs] + prev2 * cw_ref[0:1, cs] + prev1 * cw_ref[1:2, cs] + gate * cw_ref[2:3, cs]
        hidden.append(_bf(_gelu_times(gc, up)))
        if len(hidden) == FFN_OUT_GROUP or c == n_blocks - 1:
            lo = (c + 1 - len(hidden)) * fc
            acc = acc + _mm(jnp.concatenate(hidden, axis=1), wo_ref[lo:(c + 1) * fc, :])
            hidden = []
    out_ref[...] = x1 + _rms(acc, vec_ref[2:3, :])


def _mix_ffn(x, o, seq_len, w_proj, vec, w_in, conv, w_out):
    m, d = x.shape
    tm = FFN_ROW_TILE
    f = w_out.shape[0]
    return pl.pallas_call(
        functools.partial(_mix_ffn_kernel, seq_len // tm),
        grid=(m // tm,),
        in_specs=[pl.BlockSpec((tm, d), lambda i: (i, 0)),
                  pl.BlockSpec((tm, o.shape[1]), lambda i: (i, 0)),
                  _resident(w_proj.shape), _resident(vec.shape), _resident(w_in.shape),
                  _resident(conv.shape), _resident(w_out.shape)],
        out_specs=pl.BlockSpec((tm, d), lambda i: (i, 0)),
        out_shape=jax.ShapeDtypeStruct((m, d), F32),
        scratch_shapes=[pltpu.VMEM((SUBLANES, f), F32)],
        compiler_params=_params("arbitrary"),
        name="mix_ffn",
    )(x, o, w_proj, vec, w_in, conv, w_out)


def _kv_prep_kernel(x_ref, g_ref, ga_ref, wd_ref, wr_ref, cs_ref, wk_ref, wvt_ref, k_ref, vt_ref):
    xn = _bf(_rms(x_ref[...], g_ref[...]))
    ckv = _bf(_rms(_mm(xn, wd_ref[...]), ga_ref[...]))
    kr2 = _mm(xn, wr_ref[...]) * cs_ref[...]
    kr = kr2[:, :LANES] + kr2[:, LANES:]
    kn = _mm(ckv, wk_ref[...])
    for h in range(kn.shape[1] // LANES):
        sl = slice(h * LANES, (h + 1) * LANES)
        k_ref[:, sl] = _bf(kn[:, sl] + kr)
    vt = _mm_nt(wvt_ref[...], ckv)
    ones_row = (lax.broadcasted_iota(jnp.int32, vt.shape, 0) & (LANES - 1)) == HEAD
    vt_ref[...] = _bf(jnp.where(ones_row, 1.0, vt))


def _kv_prep(x, batch, seq_len, g, ga, wd, wr, cs, wk, wvt):
    m, d = x.shape
    tm = ATT_TILE
    nt = seq_len // tm
    return pl.pallas_call(
        _kv_prep_kernel,
        grid=(m // tm,),
        in_specs=[pl.BlockSpec((tm, d), lambda i: (i, 0)),
                  _full(g.shape), _full(ga.shape), _full(wd.shape), _full(wr.shape),
                  pl.BlockSpec((tm, 2 * LANES), lambda i: (i % nt, 0)),
                  _full(wk.shape), _full(wvt.shape)],
        out_specs=[pl.BlockSpec((tm, wk.shape[1]), lambda i: (i, 0)),
                   pl.BlockSpec((None, None, wvt.shape[0], tm), lambda i: (i // nt, i % nt, 0, 0))],
        out_shape=[jax.ShapeDtypeStruct((m, wk.shape[1]), BF16),
                   jax.ShapeDtypeStruct((batch, nt, wvt.shape[0], tm), BF16)],
        compiler_params=_params("parallel"),
        name="mla_kv_prep",
    )(x, g, ga, wd, wr, cs, wk, wvt)


def _q_prep_kernel(scale, x_ref, g_ref, gq_ref, wd_ref, wa_ref, tab_ref, q_ref):
    xn = _bf(_rms(x_ref[...], g_ref[...]))
    cq = _bf(_rms(_mm(xn, wd_ref[...]), gq_ref[...]))
    qa = _mm(cq, wa_ref[...])
    tab = tab_ref[...] * scale
    for h in range(qa.shape[1] // LANES):
        sl = slice(h * LANES, (h + 1) * LANES)
        q_ref[:, sl] = _bf(qa[:, sl] * tab)


def _q_prep(x, seq_len, scale, g, gq, wd, wa, tab):
    m, d = x.shape
    tm = ROW_TILE
    nt = seq_len // tm
    return pl.pallas_call(
        functools.partial(_q_prep_kernel, scale),
        grid=(m // tm,),
        in_specs=[pl.BlockSpec((tm, d), lambda i: (i, 0)),
                  _full(g.shape), _full(gq.shape), _full(wd.shape), _full(wa.shape),
                  pl.BlockSpec((tm, LANES), lambda i: (i % nt, 0))],
        out_specs=pl.BlockSpec((tm, wa.shape[1]), lambda i: (i, 0)),
        out_shape=jax.ShapeDtypeStruct((m, wa.shape[1]), BF16),
        compiler_params=_params("parallel"),
        name="mla_q_prep",
    )(x, g, gq, wd, wa, tab)


def _attn_kernel(q_ref, k_ref, vt_ref, o_ref):
    tq = q_ref.shape[0]
    tk = vt_ref.shape[-1]
    tc = ATT_Q_BLOCK
    n_blocks = tq // tc
    tiles_per_q = tq // tk
    qi = pl.program_id(2)
    streams = [(h, c) for h in range(2) for c in range(n_blocks)]
    hsl = [slice(h * LANES, (h + 1) * LANES) for h in range(2)]
    qs = [q_ref[c * tc:(c + 1) * tc, hsl[h]] for h, c in streams]

    def step(s, vt, m_i, acc):
        m_new = jnp.maximum(m_i, jnp.max(s, axis=0, keepdims=True))
        alpha = jnp.exp2(m_i - m_new)
        p = jnp.exp2(s - m_new)
        return m_new, alpha * acc + _mm(vt, _bf(p))

    def body(j, carry):
        off = pl.multiple_of(j * tk, tk)
        kt = [k_ref[pl.ds(off, tk), sl] for sl in hsl]
        ss = [_mm_nt(kt[h], q) for (h, _), q in zip(streams, qs)]
        return tuple(step(s, vt_ref[j, hsl[h], :], *mc) for (h, _), s, mc in zip(streams, ss, carry))

    init = tuple((jnp.full((1, tc), NEG_INF, F32), jnp.zeros((LANES, tc), F32)) for _ in streams)
    carry = lax.fori_loop(0, qi * tiles_per_q, body, init)

    off = pl.multiple_of(qi * tq, tq)
    outs = []
    kpos = lax.broadcasted_iota(jnp.int32, (tc, tc), 0) // MASK_CHUNK
    qpos = lax.broadcasted_iota(jnp.int32, (tc, tc), 1) // MASK_CHUNK
    visible = kpos <= qpos
    ss = [_mm_nt(k_ref[pl.ds(off, (c + 1) * tc), hsl[h]], q) for (h, c), q in zip(streams, qs)]
    for (h, c), s, mc in zip(streams, ss, carry):
        nk = (c + 1) * tc
        s_last = jnp.where(visible, s[nk - tc:], NEG_INF)
        s = s_last if c == 0 else jnp.concatenate([s[:nk - tc], s_last], axis=0)
        vt = [vt_ref[qi * tiles_per_q + t, hsl[h], :min(tk, nk - t * tk)] for t in range(-(-nk // tk))]
        _, acc = step(s, vt[0] if len(vt) == 1 else jnp.concatenate(vt, axis=1), *mc)
        outs.append(acc[:HEAD] / acc[HEAD:HEAD + 1])
    o_t = jnp.concatenate([jnp.concatenate(outs[h * n_blocks:(h + 1) * n_blocks], axis=1)
                           for h in range(2)], axis=0)
    o_ref[...] = _bf(o_t.T)


def _attention(q, k, vt, batch, seq_len):
    m = q.shape[0]
    nkt, rows, tk = vt.shape[1:]
    n_pairs = rows // (2 * LANES)
    tq = ATT_Q_TILE
    nq = seq_len // tq
    return pl.pallas_call(
        _attn_kernel,
        grid=(batch, n_pairs, nq),
        in_specs=[pl.BlockSpec((tq, 2 * LANES), lambda b, p, i: (b * nq + i, p)),
                  pl.BlockSpec((seq_len, 2 * LANES), lambda b, p, i: (b, p)),
                  pl.BlockSpec((None, nkt, 2 * LANES, tk), lambda b, p, i: (b, 0, p, 0))],
        out_specs=pl.BlockSpec((tq, PAIR_W), lambda b, p, i: (b * nq + i, p)),
        out_shape=jax.ShapeDtypeStruct((m, n_pairs * PAIR_W), BF16),
        compiler_params=_params("parallel", "parallel", "arbitrary"),
        name="mla_attention",
    )(q, k, vt)


def _pad_cols(w, n):
    return jnp.pad(w, ((0, 0), (0, n - w.shape[1])))


def _pad_rows(w, n):
    return jnp.pad(w, ((0, n - w.shape[0]), (0, 0)))


def _rows(vectors, d):
    rows = jnp.stack([v.reshape(d).astype(F32) for v in vectors])
    return _pad_rows(rows, -(-rows.shape[0] // SUBLANES) * SUBLANES)


def _rotate_half_cols(w):
    half = w.shape[1] // 2
    return jnp.concatenate([-w[:, half:], w[:, :half]], axis=1)


def _rope_tables(seq_len):
    inv = 1.0 / (ROPE_THETA ** (jnp.arange(0, QK_ROPE, 2, dtype=F32) / QK_ROPE))
    ang = jnp.arange(seq_len, dtype=F32)[:, None] * inv[None, :]
    cos, sin = jnp.cos(ang), jnp.sin(ang)
    pad = jnp.zeros((seq_len, QK_NOPE), F32)
    k_tab = jnp.concatenate([pad, cos, cos, cos, cos, pad, sin, sin, sin, sin], axis=1)
    q_tab = jnp.concatenate([jnp.ones((seq_len, QK_NOPE), F32), cos, cos, sin, sin], axis=1)
    return k_tab, q_tab


def kernel(x, norm_g, ffn_w_in, ffn_conv_w, ffn_conv_b, ffn_w_out, a_mu, a_w_rkv, a_w0, a_w1, a_w2, a_a0, a_a1, a_a2, a_g1, a_g2, a_k_k, a_k_a, a_r_k, a_lnx_w, a_lnx_b, a_w_o, kv_norm_g, kv_w_down, kv_a_norm_g, kv_w_up, q_w_down, q_norm_g, q_w_up, o_w):
    batch, seq_len, d = x.shape
    n_heads = d // HEAD
    n_a = a_mu.shape[0]
    depth = norm_g.shape[0]
    xf = x.reshape(batch * seq_len, d)

    head_of = jnp.arange(d) // HEAD
    e = _bf(head_of[:, None] == jnp.arange(LANES)[None, :])
    et = e.T
    pair_of = jnp.arange(PAIR_W) // HEAD
    bd = _bf(pair_of[:, None] == pair_of[None, :])
    k_tab, q_tab = _rope_tables(seq_len)

    kq = vq = None
    for layer in range(depth):
        gl = norm_g[layer].astype(F32)
        if layer < n_a:
            i = layer
            vec = _rows([gl[0]] + [a_mu[i, n] for n in range(6)]
                        + [a_w0[i], a_a0[i], a_k_k[i], a_k_a[i]], d)
            lora = LANES
            gate_lora = 2 * LANES
            r, k, v, a, b, g, lw = _rwkv_prep(
                xf, seq_len, vec, _bf(a_w_rkv[i]),
                _bf(_pad_cols(a_w1[i], lora)), _bf(_pad_rows(a_w2[i], lora)),
                _bf(_pad_cols(a_a1[i], lora)), _bf(_pad_rows(a_a2[i], lora)),
                _bf(_pad_cols(a_g1[i], gate_lora)), _bf(_pad_rows(a_g2[i], gate_lora)), e, et)
            svec = _rows([a_lnx_w[i], a_lnx_b[i], a_r_k[i]], d)
            mix = _rwkv_scan(r, k, v, a, b, g, lw, svec, bd, batch, seq_len)
            w_out_proj = _bf(a_w_o[i])
        else:
            if layer == n_a:
                wd = kv_w_down[:, :KV_LORA]
                wr = kv_w_down[:, KV_LORA:]
                up = kv_w_up.reshape(KV_LORA, n_heads, 2 * HEAD)
                wk = _pad_cols(up[:, :, :QK_NOPE].reshape(KV_LORA * n_heads, QK_NOPE), LANES)
                wk = wk.reshape(KV_LORA, n_heads * LANES)
                wvt = _pad_cols(up[:, :, QK_NOPE:].reshape(KV_LORA * n_heads, HEAD), LANES)
                wvt = wvt.reshape(KV_LORA, n_heads * LANES).T
                nope_pad = jnp.zeros((d, QK_NOPE), F32)
                wr_rot = _rotate_half_cols(wr)
                wr2 = jnp.concatenate([nope_pad, wr, wr, nope_pad, wr_rot, wr_rot], axis=1)
                kq, vq = _kv_prep(
                    xf, batch, seq_len, kv_norm_g.reshape(1, d), kv_a_norm_g.reshape(1, KV_LORA),
                    _bf(wd), _bf(wr2), k_tab, _bf(wk), _bf(wvt))
            j = layer - n_a
            qup = q_w_up[j].reshape(-1, n_heads, QK_NOPE + QK_ROPE)
            q_lora = qup.shape[0]
            wa = jnp.concatenate([qup, -qup[:, :, QK_NOPE + QK_ROPE // 2:],
                                  qup[:, :, QK_NOPE:QK_NOPE + QK_ROPE // 2]], axis=-1).reshape(q_lora, n_heads * LANES)
            q = _q_prep(xf, seq_len, math.log2(math.e) / math.sqrt(QK_NOPE + QK_ROPE), gl[0:1], q_norm_g[j].reshape(1, -1),
                        _bf(q_w_down[j]), _bf(wa), q_tab)
            mix = _attention(q, kq, vq, batch, seq_len)
            w_out_proj = _bf(o_w[j])
        d_ff = ffn_conv_b.shape[1]
        conv = _rows([ffn_conv_w[layer, 0], ffn_conv_w[layer, 1], ffn_conv_w[layer, 2], ffn_conv_b[layer]], d_ff)
        xf = _mix_ffn(xf, mix, seq_len, w_out_proj, _rows([gl[1], gl[2], gl[3]], d),
                      _bf(ffn_w_in[layer]), conv, _bf(ffn_w_out[layer]))
    return xf.reshape(batch, seq_len, d)
```

```python
import functools
import math

import jax
import jax.numpy as jnp
from jax import lax
from jax.experimental import pallas as pl
from jax.experimental.pallas import tpu as pltpu

F32 = jnp.float32
BF16 = jnp.bfloat16

HEAD = 64
LNX_EPS = 64e-5
NORM_EPS = 1e-6
QK_NOPE = 64
QK_ROPE = 32
KV_LORA = 256
ROPE_THETA = 10000.0
MASK_CHUNK = 64
NEG_INF = -1e30

LANES = 128
SUBLANES = 8
VMEM_LIMIT = 48 * 1024 * 1024

SCAN_CHUNK = 64
SCAN_BATCH = 4
ROW_TILE = 512
FFN_ROW_TILE = 512
FFN_COL_BLOCK = 256
FFN_LOOKAHEAD = 2
FFN_OUT_GROUP = 6
ATT_TILE = 1024
ATT_Q_TILE = 2048
ATT_Q_BLOCK = 256


def _bf(x):
    return x.astype(BF16)


def _mm(a, b):
    return jnp.dot(a, b, preferred_element_type=F32)


def _mm_nt(a, b):
    return lax.dot_general(a, b, (((1,), (1,)), ((), ())), preferred_element_type=F32)


def _mm_tn(a, b):
    return lax.dot_general(a, b, (((0,), (0,)), ((), ())), preferred_element_type=F32)


def _rms(x, g):
    return x * lax.rsqrt(jnp.mean(x * x, axis=-1, keepdims=True) + NORM_EPS) * g


def _sigmoid(x):
    return 1.0 / (1.0 + jnp.exp2(x * -math.log2(math.e)))


def _params(*sem):
    return pltpu.CompilerParams(dimension_semantics=sem, vmem_limit_bytes=VMEM_LIMIT)


def _full(shape):
    nd = len(shape)
    return pl.BlockSpec(shape, lambda *_: (0,) * nd)


def _resident(shape):
    nd = len(shape)
    return pl.BlockSpec(shape, lambda *_: (0,) * nd, pipeline_mode=pl.Buffered(1))


def _rwkv_prep_kernel(seq_len, x_ref, xh_ref, vec_ref, wrkv_ref, w1_ref, w2_ref, a1_ref, a2_ref,
                      g1_ref, g2_ref, e_ref, et_ref,
                      r_ref, k_ref, v_ref, a_ref, b_ref, g_ref, lw_ref, h_ref, xx_ref):
    tm = x_ref.shape[0]
    gn = vec_ref[0:1, :]
    h = _rms(x_ref[...], gn)
    h_halo = _rms(xh_ref[SUBLANES - 1:SUBLANES, :], gn)
    at_start = (pl.program_id(0) * tm) % seq_len == 0
    h_halo = jnp.where(at_start, 0.0, h_halo)
    row = lax.broadcasted_iota(jnp.int32, h.shape, 0)
    h_prev = jnp.where(row == 0, h_halo, pltpu.roll(h, 1, 0))
    h_ref[...] = h
    xx_ref[...] = h_prev - h

    def mix(i):
        return _bf(h_ref[...] + xx_ref[...] * vec_ref[1 + i:2 + i, :])

    r = _mm(mix(0), wrkv_ref[0])
    k = _mm(mix(1), wrkv_ref[1])
    v = _mm(mix(2), wrkv_ref[2])
    zw = vec_ref[7:8, :] + _mm(_bf(jnp.tanh(_mm(mix(3), w1_ref[...]))), w2_ref[...])
    lw = -math.exp(-0.5) * _sigmoid(zw)
    a = _sigmoid(vec_ref[8:9, :] + _mm(_bf(_mm(mix(4), a1_ref[...])), a2_ref[...]))
    g = _mm(_bf(_sigmoid(_mm(mix(5), g1_ref[...]))), g2_ref[...])
    kk = k * vec_ref[9:10, :]
    ss = _mm(_bf(_mm(_bf(kk * kk), e_ref[...])), et_ref[...])
    kk = kk * lax.rsqrt(jnp.maximum(ss, 1e-24))
    k = k * (1.0 + (a - 1.0) * vec_ref[10:11, :])
    r_ref[...] = _bf(r)
    k_ref[...] = _bf(k)
    v_ref[...] = _bf(v)
    a_ref[...] = _bf(-kk)
    b_ref[...] = _bf(kk * a)
    g_ref[...] = _bf(g)
    lw_ref[...] = lw


def _rwkv_prep(x, seq_len, vec, wrkv, w1, w2, a1, a2, g1, g2, e, et):
    m, d = x.shape
    tm = ROW_TILE
    halo = tm // SUBLANES
    row_spec = pl.BlockSpec((tm, d), lambda i: (i, 0))
    out_bf = jax.ShapeDtypeStruct((m, d), BF16)
    return pl.pallas_call(
        functools.partial(_rwkv_prep_kernel, seq_len),
        grid=(m // tm,),
        in_specs=[row_spec,
                  pl.BlockSpec((SUBLANES, d), lambda i: (jnp.maximum(i * halo - 1, 0), 0)),
                  _full(vec.shape), _full(wrkv.shape), _full(w1.shape), _full(w2.shape),
                  _full(a1.shape), _full(a2.shape), _full(g1.shape), _full(g2.shape),
                  _full(e.shape), _full(et.shape)],
        out_specs=[row_spec] * 7,
        out_shape=[out_bf] * 6 + [jax.ShapeDtypeStruct((m, d), F32)],
        scratch_shapes=[pltpu.VMEM((tm, d), F32), pltpu.VMEM((tm, d), F32)],
        compiler_params=_params("parallel"),
        name="rwkv_prep",
    )(x, x, vec, wrkv, w1, w2, a1, a2, g1, g2, e, et)


PAIR_W = 2 * HEAD
HEAD_LOG2 = HEAD.bit_length() - 1
INV_BASE_LOG2 = 3


def _unit_lower_inverse(nms, row, col):
    diff = row ^ col
    eye = jnp.where(row == col, 1.0, 0.0)
    n8 = [jnp.where((diff >> INV_BASE_LOG2) == 0, nm, 0.0) for nm in nms]
    n8b = [_bf(n) for n in n8]
    n8_2 = [_mm(n, n) for n in n8b]
    n8_2b = [_bf(n) for n in n8_2]
    size = nms[0].shape[0]
    n43 = [_mm(jnp.concatenate([n2, n], axis=0), n2) for n, n2 in zip(n8b, n8_2b)]
    xs = [eye + n + n2 + m[size:] for n, n2, m in zip(n8, n8_2, n43)]
    xs = [x + _mm(_bf(x), _bf(m[:size])) for x, m in zip(xs, n43)]
    for shift in range(INV_BASE_LOG2, HEAD_LOG2):
        s = 1 << shift
        starts = range(0, size, 2 * s)

        def lower(m):
            return jnp.concatenate([m[b + s:b + 2 * s] for b in starts], axis=0)

        offs = [_bf(jnp.where((diff >> shift) == 1, nm, 0.0)) for nm in nms]
        xbs = [_bf(x) for x in xs]
        ts = [_bf(_mm(_bf(lower(x)), off)) for x, off in zip(xs, offs)]
        upd = [_mm(t, xb) for t, xb in zip(ts, xbs)]
        xs = [jnp.concatenate([piece for i, b in enumerate(starts)
                               for piece in (x[b:b + s], x[b + s:b + 2 * s] + u[i * s:(i + 1) * s])], axis=0)
              for x, u in zip(xs, upd)]
    return xs


def _rwkv_scan_kernel(r_ref, k_ref, v_ref, a_ref, b_ref, g_ref, lw_ref, vec_ref, bd_ref,
                      o_ref, s_ref):
    nb, chunk, d = lw_ref.shape

    @pl.when(pl.program_id(1) == 0)
    def _():
        s_ref[...] = jnp.zeros_like(s_ref)

    trow = lax.broadcasted_iota(jnp.int32, (chunk, chunk), 0)
    tcol = lax.broadcasted_iota(jnp.int32, (chunk, chunk), 1)
    tri = _bf(jnp.where(trow >= tcol, 1.0, 0.0))

    def decayed(j):
        lw = lw_ref[j]
        lw_hi = _bf(lw)
        lw_lo = _bf(lw - lw_hi.astype(F32))
        c = _mm(tri, lw_hi) + _mm(tri, lw_lo)
        c_last = c[chunk - 1:chunk, :]
        r = r_ref[j].astype(F32)
        k = k_ref[j].astype(F32)
        a = a_ref[j].astype(F32)
        b = b_ref[j].astype(F32)
        e_inv = jnp.exp(-c)
        e_rem = jnp.exp(c_last - c)
        at = a * jnp.exp(c - lw)
        rt = r * jnp.exp(c)
        return dict(r=r, k=k, at=at, rt=rt, bt=b * e_inv, kt=k * e_inv, bh=_bf(b * e_rem), kh=_bf(k * e_rem),
                    g_last=jnp.exp(c_last), atb=_bf(at), rtb=_bf(rt))

    rows = [decayed(j) for j in range(nb)]

    row = lax.broadcasted_iota(jnp.int32, (PAIR_W, PAIR_W), 0)
    col = lax.broadcasted_iota(jnp.int32, (PAIR_W, PAIR_W), 1)
    strict = (row & (HEAD - 1)) > (col & (HEAD - 1))
    incl = (row & (HEAD - 1)) >= (col & (HEAD - 1))
    same_head = (row >> HEAD_LOG2) == (col >> HEAD_LOG2)
    first_half = lax.broadcasted_iota(jnp.int32, (chunk, PAIR_W), 1) < HEAD
    bd = bd_ref[...]

    def stack(x):
        return jnp.concatenate([_bf(jnp.where(first_half, x, 0.0)), _bf(jnp.where(first_half, 0.0, x))], axis=0)

    n_pairs = d // PAIR_W
    items = [(j, slice(i * PAIR_W, (i + 1) * PAIR_W)) for j in range(nb) for i in range(n_pairs)]
    n_items = len(items)

    def op(name):
        return [rows[j][name][:, sl] for j, sl in items]

    lhs = [jnp.concatenate([stack(x), stack(y_)], axis=0) for x, y_ in zip(op("at"), op("rt"))]
    rhs = [jnp.concatenate([stack(x), stack(y_)], axis=0) for x, y_ in zip(op("bt"), op("kt"))]
    p = [_mm_nt(l, r_) for l, r_ in zip(lhs, rhs)]
    n_ab = [jnp.where(strict, x[:PAIR_W, :PAIR_W], 0.0) for x in p]
    a_ak = [_bf(jnp.where(strict, x[:PAIR_W, PAIR_W:], 0.0)) for x in p]
    aa = [jnp.concatenate([_bf(jnp.where(incl, x[PAIR_W:, :PAIR_W], 0.0)),
                           _bf(jnp.where(incl, x[PAIR_W:, PAIR_W:], 0.0))], axis=1) for x in p]
    tinv = [_bf(x) for x in _unit_lower_inverse(n_ab, row, col)]
    tt = [jnp.concatenate([t, _bf(_mm(t, ak))], axis=1) for t, ak in zip(tinv, a_ak)]

    s0 = [s_ref[i] for i in range(n_items)]
    vp = [v_ref[j, :, sl] for j, sl in items]
    qq = [_mm_nt(jnp.concatenate([x, y_], axis=0), _bf(s)) for x, y_, s in zip(op("atb"), op("rtb"), s0)]
    qa = [_bf(x[:chunk]) for x in qq]
    qr = [x[chunk:] for x in qq]
    u_st = [_mm(tt[i], jnp.concatenate([qa[i], qa[i], vp[i], vp[i]], axis=0)) for i in range(n_items)]
    y_st = [jnp.concatenate([qr[i], qr[i]], axis=0)
            + _mm(aa[i], jnp.concatenate([_bf(u_st[i]), vp[i], vp[i]], axis=0)) for i in range(n_items)]
    u = [jnp.where(first_half, x[:chunk], x[chunk:]) for x in u_st]
    y = [jnp.where(first_half, x[:chunk], x[chunk:]) for x in y_st]
    ds = [_mm_tn(jnp.concatenate([_bf(u[i]), vp[i]], axis=0), jnp.concatenate([x, y_], axis=0))
          for i, (x, y_) in enumerate(zip(op("bh"), op("kh")))]
    for i, g_last in enumerate(op("g_last")):
        s_ref[i] = s0[i] * g_last + jnp.where(same_head, ds[i], 0.0)

    def head_sums(xs):
        tot = _mm(jnp.concatenate([_bf(x) for x in xs], axis=0), bd)
        return [tot[i * chunk:(i + 1) * chunk] for i in range(n_items)]

    mean = [x * (1.0 / HEAD) for x in head_sums(y)]
    dev = [x - m_ for x, m_ in zip(y, mean)]
    var = [x * (1.0 / HEAD) for x in head_sums([x * x for x in dev])]
    bonus = head_sums([r_ * k_ * vec_ref[2:3, sl] for r_, k_, (_, sl) in zip(op("r"), op("k"), items)])
    for i, (j, sl) in enumerate(items):
        yn = dev[i] * lax.rsqrt(var[i] + LNX_EPS) * vec_ref[0:1, sl] + vec_ref[1:2, sl]
        o_ref[j, :, sl] = _bf((yn + bonus[i] * vp[i].astype(F32)) * g_ref[j, :, sl].astype(F32))


def _rwkv_scan(r, k, v, a, b, g, lw, vec, bd, batch, seq_len):
    m, d = lw.shape
    nb = SCAN_BATCH
    blk = pl.BlockSpec((nb, SCAN_CHUNK, d), lambda bi, ci: (bi, ci, 0))
    seq = [x.reshape(batch, seq_len, d) for x in (r, k, v, a, b, g, lw)]
    out = pl.pallas_call(
        _rwkv_scan_kernel,
        grid=(batch // nb, seq_len // SCAN_CHUNK),
        in_specs=[blk] * 7 + [_full(vec.shape), _full(bd.shape)],
        out_specs=blk,
        out_shape=jax.ShapeDtypeStruct((batch, seq_len, d), BF16),
        scratch_shapes=[pltpu.VMEM((nb * d // PAIR_W, PAIR_W, PAIR_W), F32)],
        compiler_params=_params("parallel", "arbitrary"),
        name="rwkv_scan",
    )(*seq, vec, bd)
    return out.reshape(m, d)


def _gelu_times(x, y):
    k1 = -2.0 * math.sqrt(2.0 / math.pi) * math.log2(math.e)
    e = jnp.exp2(x * (k1 + (k1 * 0.044715) * (x * x)))
    return (x * y) / (1.0 + e)


def _mix_ffn_kernel(tiles_per_seq, x_ref, o_ref, wp_ref, vec_ref, win_ref, cw_ref, wo_ref,
                    out_ref, tail_ref):
    tm = x_ref.shape[0]
    fc = FFN_COL_BLOCK
    x1 = x_ref[...] + _rms(_mm(o_ref[...], wp_ref[...]), vec_ref[0:1, :])
    xn = _bf(_rms(x1, vec_ref[1:2, :]))

    @pl.when(pl.program_id(0) % tiles_per_seq == 0)
    def _():
        tail_ref[...] = jnp.zeros_like(tail_ref)

    row = lax.broadcasted_iota(jnp.int32, (tm, fc), 0)
    acc = jnp.zeros(x1.shape, F32)
    f = win_ref.shape[1] // 2
    n_blocks = f // fc

    def gate_up(c):
        return _mm(xn, win_ref[:, c * fc:(c + 1) * fc]), _mm(xn, win_ref[:, f + c * fc:f + (c + 1) * fc])

    ahead = [gate_up(c) for c in range(min(FFN_LOOKAHEAD, n_blocks))]
    hidden = []
    for c in range(n_blocks):
        cs = slice(c * fc, (c + 1) * fc)
        gate, up = ahead.pop(0)
        if c + FFN_LOOKAHEAD < n_blocks:
            ahead.append(gate_up(c + FFN_LOOKAHEAD))
        t1 = tail_ref[SUBLANES - 1:SUBLANES, cs]
        t2 = tail_ref[SUBLANES - 2:SUBLANES - 1, cs]
        prev1 = jnp.where(row == 0, t1, pltpu.roll(gate, 1, 0))
        prev2 = jnp.where(row == 0, t2, jnp.where(row == 1, t1, pltpu.roll(gate, 2, 0)))
        tail_ref[:, cs] = gate[tm - SUBLANES:, :]
        gc = cw_ref[3:4, cs] + prev2 * cw_ref[0:1, cs] + prev1 * cw_ref[1:2, cs] + gate * cw_ref[2:3, cs]
        hidden.append(_bf(_gelu_times(gc, up)))
        if len(hidden) == FFN_OUT_GROUP or c == n_blocks - 1:
            lo = (c + 1 - len(hidden)) * fc
            acc = acc + _mm(jnp.concatenate(hidden, axis=1), wo_ref[lo:(c + 1) * fc, :])
            hidden = []
    out_ref[...] = x1 + _rms(acc, vec_ref[2:3, :])


def _mix_ffn(x, o, seq_len, w_proj, vec, w_in, conv, w_out):
    m, d = x.shape
    tm = FFN_ROW_TILE
    f = w_out.shape[0]
    return pl.pallas_call(
        functools.partial(_mix_ffn_kernel, seq_len // tm),
        grid=(m // tm,),
        in_specs=[pl.BlockSpec((tm, d), lambda i: (i, 0)),
                  pl.BlockSpec((tm, o.shape[1]), lambda i: (i, 0)),
                  _resident(w_proj.shape), _resident(vec.shape), _resident(w_in.shape),
                  _resident(conv.shape), _resident(w_out.shape)],
        out_specs=pl.BlockSpec((tm, d), lambda i: (i, 0)),
        out_shape=jax.ShapeDtypeStruct((m, d), F32),
        scratch_shapes=[pltpu.VMEM((SUBLANES, f), F32)],
        compiler_params=_params("arbitrary"),
        name="mix_ffn",
    )(x, o, w_proj, vec, w_in, conv, w_out)


def _kv_prep_kernel(x_ref, g_ref, ga_ref, wd_ref, wr_ref, cs_ref, wk_ref, wvt_ref, k_ref, vt_ref):
    xn = _bf(_rms(x_ref[...], g_ref[...]))
    ckv = _bf(_rms(_mm(xn, wd_ref[...]), ga_ref[...]))
    kr2 = _mm(xn, wr_ref[...]) * cs_ref[...]
    kr = kr2[:, :LANES] + kr2[:, LANES:]
    kn = _mm(ckv, wk_ref[...])
    for h in range(kn.shape[1] // LANES):
        sl = slice(h * LANES, (h + 1) * LANES)
        k_ref[:, sl] = _bf(kn[:, sl] + kr)
    vt = _mm_nt(wvt_ref[...], ckv)
    ones_row = (lax.broadcasted_iota(jnp.int32, vt.shape, 0) & (LANES - 1)) == HEAD
    vt_ref[...] = _bf(jnp.where(ones_row, 1.0, vt))


def _kv_prep(x, batch, seq_len, g, ga, wd, wr, cs, wk, wvt):
    m, d = x.shape
    tm = ATT_TILE
    nt = seq_len // tm
    return pl.pallas_call(
        _kv_prep_kernel,
        grid=(m // tm,),
        in_specs=[pl.BlockSpec((tm, d), lambda i: (i, 0)),
                  _full(g.shape), _full(ga.shape), _full(wd.shape), _full(wr.shape),
                  pl.BlockSpec((tm, 2 * LANES), lambda i: (i % nt, 0)),
                  _full(wk.shape), _full(wvt.shape)],
        out_specs=[pl.BlockSpec((tm, wk.shape[1]), lambda i: (i, 0)),
                   pl.BlockSpec((None, None, wvt.shape[0], tm), lambda i: (i // nt, i % nt, 0, 0))],
        out_shape=[jax.ShapeDtypeStruct((m, wk.shape[1]), BF16),
                   jax.ShapeDtypeStruct((batch, nt, wvt.shape[0], tm), BF16)],
        compiler_params=_params("parallel"),
        name="mla_kv_prep",
    )(x, g, ga, wd, wr, cs, wk, wvt)


def _q_prep_kernel(scale, x_ref, g_ref, gq_ref, wd_ref, wa_ref, tab_ref, q_ref):
    xn = _bf(_rms(x_ref[...], g_ref[...]))
    cq = _bf(_rms(_mm(xn, wd_ref[...]), gq_ref[...]))
    qa = _mm(cq, wa_ref[...])
    tab = tab_ref[...] * scale
    for h in range(qa.shape[1] // LANES):
        sl = slice(h * LANES, (h + 1) * LANES)
        q_ref[:, sl] = _bf(qa[:, sl] * tab)


def _q_prep(x, seq_len, scale, g, gq, wd, wa, tab):
    m, d = x.shape
    tm = ROW_TILE
    nt = seq_len // tm
    return pl.pallas_call(
        functools.partial(_q_prep_kernel, scale),
        grid=(m // tm,),
        in_specs=[pl.BlockSpec((tm, d), lambda i: (i, 0)),
                  _full(g.shape), _full(gq.shape), _full(wd.shape), _full(wa.shape),
                  pl.BlockSpec((tm, LANES), lambda i: (i % nt, 0))],
        out_specs=pl.BlockSpec((tm, wa.shape[1]), lambda i: (i, 0)),
        out_shape=jax.ShapeDtypeStruct((m, wa.shape[1]), BF16),
        compiler_params=_params("parallel"),
        name="mla_q_prep",
    )(x, g, gq, wd, wa, tab)


def _attn_kernel(q_ref, k_ref, vt_ref, o_ref):
    tq = q_ref.shape[0]
    tk = vt_ref.shape[-1]
    tc = ATT_Q_BLOCK
    n_blocks = tq // tc
    tiles_per_q = tq // tk
    qi = pl.program_id(2)
    streams = [(h, c) for h in range(2) for c in range(n_blocks)]
    hsl = [slice(h * LANES, (h + 1) * LANES) for h in range(2)]
    qs = [q_ref[c * tc:(c + 1) * tc, hsl[h]] for h, c in streams]

    def step(s, vt, m_i, acc):
        m_new = jnp.maximum(m_i, jnp.max(s, axis=0, keepdims=True))
        alpha = jnp.exp2(m_i - m_new)
        p = jnp.exp2(s - m_new)
        return m_new, alpha * acc + _mm(vt, _bf(p))

    def body(j, carry):
        off = pl.multiple_of(j * tk, tk)
        kt = [k_ref[pl.ds(off, tk), sl] for sl in hsl]
        ss = [_mm_nt(kt[h], q) for (h, _), q in zip(streams, qs)]
        return tuple(step(s, vt_ref[j, hsl[h], :], *mc) for (h, _), s, mc in zip(streams, ss, carry))

    init = tuple((jnp.full((1, tc), NEG_INF, F32), jnp.zeros((LANES, tc), F32)) for _ in streams)
    carry = lax.fori_loop(0, qi * tiles_per_q, body, init)

    off = pl.multiple_of(qi * tq, tq)
    outs = []
    kpos = lax.broadcasted_iota(jnp.int32, (tc, tc), 0) // MASK_CHUNK
    qpos = lax.broadcasted_iota(jnp.int32, (tc, tc), 1) // MASK_CHUNK
    visible = kpos <= qpos
    ss = [_mm_nt(k_ref[pl.ds(off, (c + 1) * tc), hsl[h]], q) for (h, c), q in zip(streams, qs)]
    for (h, c), s, mc in zip(streams, ss, carry):
        nk = (c + 1) * tc
        s_last = jnp.where(visible, s[nk - tc:], NEG_INF)
        s = s_last if c == 0 else jnp.concatenate([s[:nk - tc], s_last], axis=0)
        vt = [vt_ref[qi * tiles_per_q + t, hsl[h], :min(tk, nk - t * tk)] for t in range(-(-nk // tk))]
        _, acc = step(s, vt[0] if len(vt) == 1 else jnp.concatenate(vt, axis=1), *mc)
        outs.append(acc[:HEAD] / acc[HEAD:HEAD + 1])
    o_t = jnp.concatenate([jnp.concatenate(outs[h * n_blocks:(h + 1) * n_blocks], axis=1)
                           for h in range(2)], axis=0)
    o_ref[...] = _bf(o_t.T)


def _attention(q, k, vt, batch, seq_len):
    m = q.shape[0]
    nkt, rows, tk = vt.shape[1:]
    n_pairs = rows // (2 * LANES)
    tq = ATT_Q_TILE
    nq = seq_len // tq
    return pl.pallas_call(
        _attn_kernel,
        grid=(batch, n_pairs, nq),
        in_specs=[pl.BlockSpec((tq, 2 * LANES), lambda b, p, i: (b * nq + i, p)),
                  pl.BlockSpec((seq_len, 2 * LANES), lambda b, p, i: (b, p)),
                  pl.BlockSpec((None, nkt, 2 * LANES, tk), lambda b, p, i: (b, 0, p, 0))],
        out_specs=pl.BlockSpec((tq, PAIR_W), lambda b, p, i: (b * nq + i, p)),
        out_shape=jax.ShapeDtypeStruct((m, n_pairs * PAIR_W), BF16),
        compiler_params=_params("parallel", "parallel", "arbitrary"),
        name="mla_attention",
    )(q, k, vt)


def _pad_cols(w, n):
    return jnp.pad(w, ((0, 0), (0, n - w.shape[1])))


def _pad_rows(w, n):
    return jnp.pad(w, ((0, n - w.shape[0]), (0, 0)))


def _rows(vectors, d):
    rows = jnp.stack([v.reshape(d).astype(F32) for v in vectors])
    return _pad_rows(rows, -(-rows.shape[0] // SUBLANES) * SUBLANES)


def _rotate_half_cols(w):
    half = w.shape[1] // 2
    return jnp.concatenate([-w[:, half:], w[:, :half]], axis=1)


def _rope_tables(seq_len):
    inv = 1.0 / (ROPE_THETA ** (jnp.arange(0, QK_ROPE, 2, dtype=F32) / QK_ROPE))
    ang = jnp.arange(seq_len, dtype=F32)[:, None] * inv[None, :]
    cos, sin = jnp.cos(ang), jnp.sin(ang)
    pad = jnp.zeros((seq_len, QK_NOPE), F32)
    k_tab = jnp.concatenate([pad, cos, cos, cos, cos, pad, sin, sin, sin, sin], axis=1)
    q_tab = jnp.concatenate([jnp.ones((seq_len, QK_NOPE), F32), cos, cos, sin, sin], axis=1)
    return k_tab, q_tab


def kernel(x, norm_g, ffn_w_in, ffn_conv_w, ffn_conv_b, ffn_w_out, a_mu, a_w_rkv, a_w0, a_w1, a_w2, a_a0, a_a1, a_a2, a_g1, a_g2, a_k_k, a_k_a, a_r_k, a_lnx_w, a_lnx_b, a_w_o, kv_norm_g, kv_w_down, kv_a_norm_g, kv_w_up, q_w_down, q_norm_g, q_w_up, o_w):
    batch, seq_len, d = x.shape
    n_heads = d // HEAD
    n_a = a_mu.shape[0]
    depth = norm_g.shape[0]
    xf = x.reshape(batch * seq_len, d)

    head_of = jnp.arange(d) // HEAD
    e = _bf(head_of[:, None] == jnp.arange(LANES)[None, :])
    et = e.T
    pair_of = jnp.arange(PAIR_W) // HEAD
    bd = _bf(pair_of[:, None] == pair_of[None, :])
    k_tab, q_tab = _rope_tables(seq_len)

    kq = vq = None
    for layer in range(depth):
        gl = norm_g[layer].astype(F32)
        if layer < n_a:
            i = layer
            vec = _rows([gl[0]] + [a_mu[i, n] for n in range(6)]
                        + [a_w0[i], a_a0[i], a_k_k[i], a_k_a[i]], d)
            lora = LANES
            gate_lora = 2 * LANES
            r, k, v, a, b, g, lw = _rwkv_prep(
                xf, seq_len, vec, _bf(a_w_rkv[i]),
                _bf(_pad_cols(a_w1[i], lora)), _bf(_pad_rows(a_w2[i], lora)),
                _bf(_pad_cols(a_a1[i], lora)), _bf(_pad_rows(a_a2[i], lora)),
                _bf(_pad_cols(a_g1[i], gate_lora)), _bf(_pad_rows(a_g2[i], gate_lora)), e, et)
            svec = _rows([a_lnx_w[i], a_lnx_b[i], a_r_k[i]], d)
            mix = _rwkv_scan(r, k, v, a, b, g, lw, svec, bd, batch, seq_len)
            w_out_proj = _bf(a_w_o[i])
        else:
            if layer == n_a:
                wd = kv_w_down[:, :KV_LORA]
                wr = kv_w_down[:, KV_LORA:]
                up = kv_w_up.reshape(KV_LORA, n_heads, 2 * HEAD)
                wk = _pad_cols(up[:, :, :QK_NOPE].reshape(KV_LORA * n_heads, QK_NOPE), LANES)
                wk = wk.reshape(KV_LORA, n_heads * LANES)
                wvt = _pad_cols(up[:, :, QK_NOPE:].reshape(KV_LORA * n_heads, HEAD), LANES)
                wvt = wvt.reshape(KV_LORA, n_heads * LANES).T
                nope_pad = jnp.zeros((d, QK_NOPE), F32)
                wr_rot = _rotate_half_cols(wr)
                wr2 = jnp.concatenate([nope_pad, wr, wr, nope_pad, wr_rot, wr_rot], axis=1)
                kq, vq = _kv_prep(
                    xf, batch, seq_len, kv_norm_g.reshape(1, d), kv_a_norm_g.reshape(1, KV_LORA),
                    _bf(wd), _bf(wr2), k_tab, _bf(wk), _bf(wvt))
            j = layer - n_a
            qup = q_w_up[j].reshape(-1, n_heads, QK_NOPE + QK_ROPE)
            q_lora = qup.shape[0]
            wa = jnp.concatenate([qup, -qup[:, :, QK_NOPE + QK_ROPE // 2:],
                                  qup[:, :, QK_NOPE:QK_NOPE + QK_ROPE // 2]], axis=-1).reshape(q_lora, n_heads * LANES)
            q = _q_prep(xf, seq_len, math.log2(math.e) / math.sqrt(QK_NOPE + QK_ROPE), gl[0:1], q_norm_g[j].reshape(1, -1),
                        _bf(q_w_down[j]), _bf(wa), q_tab)
            mix = _attention(q, kq, vq, batch, seq_len)
            w_out_proj = _bf(o_w[j])
        d_ff = ffn_conv_b.shape[1]
        conv = _rows([ffn_conv_w[layer, 0], ffn_conv_w[layer, 1], ffn_conv_w[layer, 2], ffn_conv_b[layer]], d_ff)
        xf = _mix_ffn(xf, mix, seq_len, w_out_proj, _rows([gl[1], gl[2], gl[3]], d),
                      _bf(ffn_w_in[layer]), conv, _bf(ffn_w_out[layer]))
    return xf.reshape(batch, seq_len, d)
```

```python
import functools
import math

import jax
import jax.numpy as jnp
from jax import lax
from jax.experimental import pallas as pl
from jax.experimental.pallas import tpu as pltpu

F32 = jnp.float32
BF16 = jnp.bfloat16

HEAD = 64
LNX_EPS = 64e-5
NORM_EPS = 1e-6
QK_NOPE = 64
QK_ROPE = 32
KV_LORA = 256
ROPE_THETA = 10000.0
MASK_CHUNK = 64
NEG_INF = -1e30

LANES = 128
SUBLANES = 8
VMEM_LIMIT = 48 * 1024 * 1024

SCAN_CHUNK = 64
SCAN_BATCH = 4
ROW_TILE = 512
FFN_ROW_TILE = 512
FFN_COL_BLOCK = 256
FFN_LOOKAHEAD = 2
FFN_OUT_GROUP = 6
ATT_TILE = 1024
ATT_Q_TILE = 2048
ATT_Q_BLOCK = 256


def _bf(x):
    return x.astype(BF16)


def _mm(a, b):
    return jnp.dot(a, b, preferred_element_type=F32)


def _mm_nt(a, b):
    return lax.dot_general(a, b, (((1,), (1,)), ((), ())), preferred_element_type=F32)


def _mm_tn(a, b):
    return lax.dot_general(a, b, (((0,), (0,)), ((), ())), preferred_element_type=F32)


def _rms(x, g):
    return x * lax.rsqrt(jnp.mean(x * x, axis=-1, keepdims=True) + NORM_EPS) * g


def _sigmoid(x):
    return 1.0 / (1.0 + jnp.exp2(x * -math.log2(math.e)))


def _params(*sem):
    return pltpu.CompilerParams(dimension_semantics=sem, vmem_limit_bytes=VMEM_LIMIT)


def _full(shape):
    nd = len(shape)
    return pl.BlockSpec(shape, lambda *_: (0,) * nd)


def _resident(shape):
    nd = len(shape)
    return pl.BlockSpec(shape, lambda *_: (0,) * nd, pipeline_mode=pl.Buffered(1))


def _rwkv_prep_kernel(seq_len, x_ref, xh_ref, vec_ref, wrkv_ref, w1_ref, w2_ref, a1_ref, a2_ref,
                      g1_ref, g2_ref, e_ref, et_ref,
                      r_ref, k_ref, v_ref, a_ref, b_ref, g_ref, lw_ref, h_ref, xx_ref):
    tm = x_ref.shape[0]
    gn = vec_ref[0:1, :]
    h = _rms(x_ref[...], gn)
    h_halo = _rms(xh_ref[SUBLANES - 1:SUBLANES, :], gn)
    at_start = (pl.program_id(0) * tm) % seq_len == 0
    h_halo = jnp.where(at_start, 0.0, h_halo)
    row = lax.broadcasted_iota(jnp.int32, h.shape, 0)
    h_prev = jnp.where(row == 0, h_halo, pltpu.roll(h, 1, 0))
    h_ref[...] = h
    xx_ref[...] = h_prev - h

    def mix(i):
        return _bf(h_ref[...] + xx_ref[...] * vec_ref[1 + i:2 + i, :])

    r = _mm(mix(0), wrkv_ref[0])
    k = _mm(mix(1), wrkv_ref[1])
    v = _mm(mix(2), wrkv_ref[2])
    zw = vec_ref[7:8, :] + _mm(_bf(jnp.tanh(_mm(mix(3), w1_ref[...]))), w2_ref[...])
    lw = -math.exp(-0.5) * _sigmoid(zw)
    a = _sigmoid(vec_ref[8:9, :] + _mm(_bf(_mm(mix(4), a1_ref[...])), a2_ref[...]))
    g = _mm(_bf(_sigmoid(_mm(mix(5), g1_ref[...]))), g2_ref[...])
    kk = k * vec_ref[9:10, :]
    ss = _mm(_bf(_mm(_bf(kk * kk), e_ref[...])), et_ref[...])
    kk = kk * lax.rsqrt(jnp.maximum(ss, 1e-24))
    k = k * (1.0 + (a - 1.0) * vec_ref[10:11, :])
    r_ref[...] = _bf(r)
    k_ref[...] = _bf(k)
    v_ref[...] = _bf(v)
    a_ref[...] = _bf(-kk)
    b_ref[...] = _bf(kk * a)
    g_ref[...] = _bf(g)
    lw_ref[...] = lw


def _rwkv_prep(x, seq_len, vec, wrkv, w1, w2, a1, a2, g1, g2, e, et):
    m, d = x.shape
    tm = ROW_TILE
    halo = tm // SUBLANES
    row_spec = pl.BlockSpec((tm, d), lambda i: (i, 0))
    out_bf = jax.ShapeDtypeStruct((m, d), BF16)
    return pl.pallas_call(
        functools.partial(_rwkv_prep_kernel, seq_len),
        grid=(m // tm,),
        in_specs=[row_spec,
                  pl.BlockSpec((SUBLANES, d), lambda i: (jnp.maximum(i * halo - 1, 0), 0)),
                  _full(vec.shape), _full(wrkv.shape), _full(w1.shape), _full(w2.shape),
                  _full(a1.shape), _full(a2.shape), _full(g1.shape), _full(g2.shape),
                  _full(e.shape), _full(et.shape)],
        out_specs=[row_spec] * 7,
        out_shape=[out_bf] * 6 + [jax.ShapeDtypeStruct((m, d), F32)],
        scratch_shapes=[pltpu.VMEM((tm, d), F32), pltpu.VMEM((tm, d), F32)],
        compiler_params=_params("parallel"),
        name="rwkv_prep",
    )(x, x, vec, wrkv, w1, w2, a1, a2, g1, g2, e, et)


PAIR_W = 2 * HEAD
HEAD_LOG2 = HEAD.bit_length() - 1
INV_BASE_LOG2 = 3


def _unit_lower_inverse(nms, row, col):
    diff = row ^ col
    eye = jnp.where(row == col, 1.0, 0.0)
    n8 = [jnp.where((diff >> INV_BASE_LOG2) == 0, nm, 0.0) for nm in nms]
    n8b = [_bf(n) for n in n8]
    n8_2 = [_mm(n, n) for n in n8b]
    n8_2b = [_bf(n) for n in n8_2]
    size = nms[0].shape[0]
    n43 = [_mm(jnp.concatenate([n2, n], axis=0), n2) for n, n2 in zip(n8b, n8_2b)]
    xs = [eye + n + n2 + m[size:] for n, n2, m in zip(n8, n8_2, n43)]
    xs = [x + _mm(_bf(x), _bf(m[:size])) for x, m in zip(xs, n43)]
    for shift in range(INV_BASE_LOG2, HEAD_LOG2):
        s = 1 << shift
        starts = range(0, size, 2 * s)

        def lower(m):
            return jnp.concatenate([m[b + s:b + 2 * s] for b in starts], axis=0)

        offs = [_bf(jnp.where((diff >> shift) == 1, nm, 0.0)) for nm in nms]
        xbs = [_bf(x) for x in xs]
        ts = [_bf(_mm(_bf(lower(x)), off)) for x, off in zip(xs, offs)]
        upd = [_mm(t, xb) for t, xb in zip(ts, xbs)]
        xs = [jnp.concatenate([piece for i, b in enumerate(starts)
                               for piece in (x[b:b + s], x[b + s:b + 2 * s] + u[i * s:(i + 1) * s])], axis=0)
              for x, u in zip(xs, upd)]
    return xs


def _rwkv_scan_kernel(r_ref, k_ref, v_ref, a_ref, b_ref, g_ref, lw_ref, vec_ref, bd_ref,
                      o_ref, s_ref):
    nb, chunk, d = lw_ref.shape

    @pl.when(pl.program_id(1) == 0)
    def _():
        s_ref[...] = jnp.zeros_like(s_ref)

    trow = lax.broadcasted_iota(jnp.int32, (chunk, chunk), 0)
    tcol = lax.broadcasted_iota(jnp.int32, (chunk, chunk), 1)
    tri = _bf(jnp.where(trow >= tcol, 1.0, 0.0))

    def decayed(j):
        lw = lw_ref[j]
        lw_hi = _bf(lw)
        lw_lo = _bf(lw - lw_hi.astype(F32))
        c = _mm(tri, lw_hi) + _mm(tri, lw_lo)
        c_last = c[chunk - 1:chunk, :]
        r = r_ref[j].astype(F32)
        k = k_ref[j].astype(F32)
        a = a_ref[j].astype(F32)
        b = b_ref[j].astype(F32)
        e_inv = jnp.exp(-c)
        e_rem = jnp.exp(c_last - c)
        at = a * jnp.exp(c - lw)
        rt = r * jnp.exp(c)
        return dict(r=r, k=k, at=at, rt=rt, bt=b * e_inv, kt=k * e_inv, bh=_bf(b * e_rem), kh=_bf(k * e_rem),
                    g_last=jnp.exp(c_last), atb=_bf(at), rtb=_bf(rt))

    rows = [decayed(j) for j in range(nb)]

    row = lax.broadcasted_iota(jnp.int32, (PAIR_W, PAIR_W), 0)
    col = lax.broadcasted_iota(jnp.int32, (PAIR_W, PAIR_W), 1)
    strict = (row & (HEAD - 1)) > (col & (HEAD - 1))
    incl = (row & (HEAD - 1)) >= (col & (HEAD - 1))
    same_head = (row >> HEAD_LOG2) == (col >> HEAD_LOG2)
    first_half = lax.broadcasted_iota(jnp.int32, (chunk, PAIR_W), 1) < HEAD
    bd = bd_ref[...]

    def stack(x):
        return jnp.concatenate([_bf(jnp.where(first_half, x, 0.0)), _bf(jnp.where(first_half, 0.0, x))], axis=0)

    n_pairs = d // PAIR_W
    items = [(j, slice(i * PAIR_W, (i + 1) * PAIR_W)) for j in range(nb) for i in range(n_pairs)]
    n_items = len(items)

    def op(name):
        return [rows[j][name][:, sl] for j, sl in items]

    lhs = [jnp.concatenate([stack(x), stack(y_)], axis=0) for x, y_ in zip(op("at"), op("rt"))]
    rhs = [jnp.concatenate([stack(x), stack(y_)], axis=0) for x, y_ in zip(op("bt"), op("kt"))]
    p = [_mm_nt(l, r_) for l, r_ in zip(lhs, rhs)]
    n_ab = [jnp.where(strict, x[:PAIR_W, :PAIR_W], 0.0) for x in p]
    a_ak = [_bf(jnp.where(strict, x[:PAIR_W, PAIR_W:], 0.0)) for x in p]
    aa = [jnp.concatenate([_bf(jnp.where(incl, x[PAIR_W:, :PAIR_W], 0.0)),
                           _bf(jnp.where(incl, x[PAIR_W:, PAIR_W:], 0.0))], axis=1) for x in p]
    tinv = [_bf(x) for x in _unit_lower_inverse(n_ab, row, col)]
    tt = [jnp.concatenate([t, _bf(_mm(t, ak))], axis=1) for t, ak in zip(tinv, a_ak)]

    s0 = [s_ref[i] for i in range(n_items)]
    vp = [v_ref[j, :, sl] for j, sl in items]
    qq = [_mm_nt(jnp.concatenate([x, y_], axis=0), _bf(s)) for x, y_, s in zip(op("atb"), op("rtb"), s0)]
    qa = [_bf(x[:chunk]) for x in qq]
    qr = [x[chunk:] for x in qq]
    u_st = [_mm(tt[i], jnp.concatenate([qa[i], qa[i], vp[i], vp[i]], axis=0)) for i in range(n_items)]
    y_st = [jnp.concatenate([qr[i], qr[i]], axis=0)
            + _mm(aa[i], jnp.concatenate([_bf(u_st[i]), vp[i], vp[i]], axis=0)) for i in range(n_items)]
    u = [jnp.where(first_half, x[:chunk], x[chunk:]) for x in u_st]
    y = [jnp.where(first_half, x[:chunk], x[chunk:]) for x in y_st]
    ds = [_mm_tn(jnp.concatenate([_bf(u[i]), vp[i]], axis=0), jnp.concatenate([x, y_], axis=0))
          for i, (x, y_) in enumerate(zip(op("bh"), op("kh")))]
    for i, g_last in enumerate(op("g_last")):
        s_ref[i] = s0[i] * g_last + jnp.where(same_head, ds[i], 0.0)

    def head_sums(xs):
        tot = _mm(jnp.concatenate([_bf(x) for x in xs], axis=0), bd)
        return [tot[i * chunk:(i + 1) * chunk] for i in range(n_items)]

    mean = [x * (1.0 / HEAD) for x in head_sums(y)]
    dev = [x - m_ for x, m_ in zip(y, mean)]
    var = [x * (1.0 / HEAD) for x in head_sums([x * x for x in dev])]
    bonus = head_sums([r_ * k_ * vec_ref[2:3, sl] for r_, k_, (_, sl) in zip(op("r"), op("k"), items)])
    for i, (j, sl) in enumerate(items):
        yn = dev[i] * lax.rsqrt(var[i] + LNX_EPS) * vec_ref[0:1, sl] + vec_ref[1:2, sl]
        o_ref[j, :, sl] = _bf((yn + bonus[i] * vp[i].astype(F32)) * g_ref[j, :, sl].astype(F32))


def _rwkv_scan(r, k, v, a, b, g, lw, vec, bd, batch, seq_len):
    m, d = lw.shape
    nb = SCAN_BATCH
    blk = pl.BlockSpec((nb, SCAN_CHUNK, d), lambda bi, ci: (bi, ci, 0))
    seq = [x.reshape(batch, seq_len, d) for x in (r, k, v, a, b, g, lw)]
    out = pl.pallas_call(
        _rwkv_scan_kernel,
        grid=(batch // nb, seq_len // SCAN_CHUNK),
        in_specs=[blk] * 7 + [_full(vec.shape), _full(bd.shape)],
        out_specs=blk,
        out_shape=jax.ShapeDtypeStruct((batch, seq_len, d), BF16),
        scratch_shapes=[pltpu.VMEM((nb * d // PAIR_W, PAIR_W, PAIR_W), F32)],
        compiler_params=_params("parallel", "arbitrary"),
        name="rwkv_scan",
    )(*seq, vec, bd)
    return out.reshape(m, d)


def _gelu_times(x, y):
    k1 = -2.0 * math.sqrt(2.0 / math.pi) * math.log2(math.e)
    e = jnp.exp2(x * (k1 + (k1 * 0.044715) * (x * x)))
    return (x * y) / (1.0 + e)


def _mix_ffn_kernel(tiles_per_seq, x_ref, o_ref, wp_ref, vec_ref, win_ref, cw_ref, wo_ref,
                    out_ref, tail_ref):
    tm = x_ref.shape[0]
    fc = FFN_COL_BLOCK
    x1 = x_ref[...] + _rms(_mm(o_ref[...], wp_ref[...]), vec_ref[0:1, :])
    xn = _bf(_rms(x1, vec_ref[1:2, :]))

    @pl.when(pl.program_id(0) % tiles_per_seq == 0)
    def _():
        tail_ref[...] = jnp.zeros_like(tail_ref)

    row = lax.broadcasted_iota(jnp.int32, (tm, fc), 0)
    acc = jnp.zeros(x1.shape, F32)
    f = win_ref.shape[1] // 2
    n_blocks = f // fc

    def gate_up(c):
        return _mm(xn, win_ref[:, c * fc:(c + 1) * fc]), _mm(xn, win_ref[:, f + c * fc:f + (c + 1) * fc])

    ahead = [gate_up(c) for c in range(min(FFN_LOOKAHEAD, n_blocks))]
    hidden = []
    for c in range(n_blocks):
        cs = slice(c * fc, (c + 1) * fc)
        gate, up = ahead.pop(0)
        if c + FFN_LOOKAHEAD < n_blocks:
            ahead.append(gate_up(c + FFN_LOOKAHEAD))
        t1 = tail_ref[SUBLANES - 1:SUBLANES, cs]
        t2 = tail_ref[SUBLANES - 2:SUBLANES - 1, cs]
        prev1 = jnp.where(row == 0, t1, pltpu.roll(gate, 1, 0))
        prev2 = jnp.where(row == 0, t2, jnp.where(row == 1, t1, pltpu.roll(gate, 2, 0)))
        tail_ref[:, cs] = gate[tm - SUBLANES:, :]
        gc = cw_ref[3:4, cs] + prev2 * cw_ref[0:1, cs] + prev1 * cw_ref[1:2, cs] + gate * cw_ref[2:3, cs]
        hidden.append(_bf(_gelu_times(gc, up)))
        if len(hidden) == FFN_OUT_GROUP or c == n_blocks - 1:
            lo = (c + 1 - len(hidden)) * fc
            acc = acc + _mm(jnp.concatenate(hidden, axis=1), wo_ref[lo:(c + 1) * fc, :])
            hidden = []
    out_ref[...] = x1 + _rms(acc, vec_ref[2:3, :])


def _mix_ffn(x, o, seq_len, w_proj, vec, w_in, conv, w_out):
    m, d = x.shape
    tm = FFN_ROW_TILE
    f = w_out.shape[0]
    return pl.pallas_call(
        functools.partial(_mix_ffn_kernel, seq_len // tm),
        grid=(m // tm,),
        in_specs=[pl.BlockSpec((tm, d), lambda i: (i, 0)),
                  pl.BlockSpec((tm, o.shape[1]), lambda i: (i, 0)),
                  _resident(w_proj.shape), _resident(vec.shape), _resident(w_in.shape),
                  _resident(conv.shape), _resident(w_out.shape)],
        out_specs=pl.BlockSpec((tm, d), lambda i: (i, 0)),
        out_shape=jax.ShapeDtypeStruct((m, d), F32),
        scratch_shapes=[pltpu.VMEM((SUBLANES, f), F32)],
        compiler_params=_params("arbitrary"),
        name="mix_ffn",
    )(x, o, w_proj, vec, w_in, conv, w_out)


def _kv_prep_kernel(x_ref, g_ref, ga_ref, wd_ref, wr_ref, cs_ref, wk_ref, wvt_ref, k_ref, vt_ref):
    xn = _bf(_rms(x_ref[...], g_ref[...]))
    ckv = _bf(_rms(_mm(xn, wd_ref[...]), ga_ref[...]))
    kr2 = _mm(xn, wr_ref[...]) * cs_ref[...]
    kr = kr2[:, :LANES] + kr2[:, LANES:]
    kn = _mm(ckv, wk_ref[...])
    for h in range(kn.shape[1] // LANES):
        sl = slice(h * LANES, (h + 1) * LANES)
        k_ref[:, sl] = _bf(kn[:, sl] + kr)
    vt = _mm_nt(wvt_ref[...], ckv)
    ones_row = (lax.broadcasted_iota(jnp.int32, vt.shape, 0) & (LANES - 1)) == HEAD
    vt_ref[...] = _bf(jnp.where(ones_row, 1.0, vt))


def _kv_prep(x, batch, seq_len, g, ga, wd, wr, cs, wk, wvt):
    m, d = x.shape
    tm = ATT_TILE
    nt = seq_len // tm
    return pl.pallas_call(
        _kv_prep_kernel,
        grid=(m // tm,),
        in_specs=[pl.BlockSpec((tm, d), lambda i: (i, 0)),
                  _full(g.shape), _full(ga.shape), _full(wd.shape), _full(wr.shape),
                  pl.BlockSpec((tm, 2 * LANES), lambda i: (i % nt, 0)),
                  _full(wk.shape), _full(wvt.shape)],
        out_specs=[pl.BlockSpec((tm, wk.shape[1]), lambda i: (i, 0)),
                   pl.BlockSpec((None, None, wvt.shape[0], tm), lambda i: (i // nt, i % nt, 0, 0))],
        out_shape=[jax.ShapeDtypeStruct((m, wk.shape[1]), BF16),
                   jax.ShapeDtypeStruct((batch, nt, wvt.shape[0], tm), BF16)],
        compiler_params=_params("parallel"),
        name="mla_kv_prep",
    )(x, g, ga, wd, wr, cs, wk, wvt)


def _q_prep_kernel(scale, x_ref, g_ref, gq_ref, wd_ref, wa_ref, tab_ref, q_ref):
    xn = _bf(_rms(x_ref[...], g_ref[...]))
    cq = _bf(_rms(_mm(xn, wd_ref[...]), gq_ref[...]))
    qa = _mm(cq, wa_ref[...])
    tab = tab_ref[...] * scale
    for h in range(qa.shape[1] // LANES):
        sl = slice(h * LANES, (h + 1) * LANES)
        q_ref[:, sl] = _bf(qa[:, sl] * tab)


def _q_prep(x, seq_len, scale, g, gq, wd, wa, tab):
    m, d = x.shape
    tm = ROW_TILE
    nt = seq_len // tm
    return pl.pallas_call(
        functools.partial(_q_prep_kernel, scale),
        grid=(m // tm,),
        in_specs=[pl.BlockSpec((tm, d), lambda i: (i, 0)),
                  _full(g.shape), _full(gq.shape), _full(wd.shape), _full(wa.shape),
                  pl.BlockSpec((tm, LANES), lambda i: (i % nt, 0))],
        out_specs=pl.BlockSpec((tm, wa.shape[1]), lambda i: (i, 0)),
        out_shape=jax.ShapeDtypeStruct((m, wa.shape[1]), BF16),
        compiler_params=_params("parallel"),
        name="mla_q_prep",
    )(x, g, gq, wd, wa, tab)


def _attn_kernel(q_ref, k_ref, vt_ref, o_ref):
    tq = q_ref.shape[0]
    tk = vt_ref.shape[-1]
    tc = ATT_Q_BLOCK
    n_blocks = tq // tc
    tiles_per_q = tq // tk
    qi = pl.program_id(2)
    streams = [(h, c) for h in range(2) for c in range(n_blocks)]
    hsl = [slice(h * LANES, (h + 1) * LANES) for h in range(2)]
    qs = [q_ref[c * tc:(c + 1) * tc, hsl[h]] for h, c in streams]

    def step(s, vt, m_i, acc):
        m_new = jnp.maximum(m_i, jnp.max(s, axis=0, keepdims=True))
        alpha = jnp.exp2(m_i - m_new)
        p = jnp.exp2(s - m_new)
        return m_new, alpha * acc + _mm(vt, _bf(p))

    def body(j, carry):
        off = pl.multiple_of(j * tk, tk)
        kt = [k_ref[pl.ds(off, tk), sl] for sl in hsl]
        ss = [_mm_nt(kt[h], q) for (h, _), q in zip(streams, qs)]
        return tuple(step(s, vt_ref[j, hsl[h], :], *mc) for (h, _), s, mc in zip(streams, ss, carry))

    init = tuple((jnp.full((1, tc), NEG_INF, F32), jnp.zeros((LANES, tc), F32)) for _ in streams)
    carry = lax.fori_loop(0, qi * tiles_per_q, body, init)

    off = pl.multiple_of(qi * tq, tq)
    outs = []
    kpos = lax.broadcasted_iota(jnp.int32, (tc, tc), 0) // MASK_CHUNK
    qpos = lax.broadcasted_iota(jnp.int32, (tc, tc), 1) // MASK_CHUNK
    visible = kpos <= qpos
    ss = [_mm_nt(k_ref[pl.ds(off, (c + 1) * tc), hsl[h]], q) for (h, c), q in zip(streams, qs)]
    for (h, c), s, mc in zip(streams, ss, carry):
        nk = (c + 1) * tc
        s_last = jnp.where(visible, s[nk - tc:], NEG_INF)
        s = s_last if c == 0 else jnp.concatenate([s[:nk - tc], s_last], axis=0)
        vt = [vt_ref[qi * tiles_per_q + t, hsl[h], :min(tk, nk - t * tk)] for t in range(-(-nk // tk))]
        _, acc = step(s, vt[0] if len(vt) == 1 else jnp.concatenate(vt, axis=1), *mc)
        outs.append(acc[:HEAD] / acc[HEAD:HEAD + 1])
    o_t = jnp.concatenate([jnp.concatenate(outs[h * n_blocks:(h + 1) * n_blocks], axis=1)
                           for h in range(2)], axis=0)
    o_ref[...] = _bf(o_t.T)


def _attention(q, k, vt, batch, seq_len):
    m = q.shape[0]
    nkt, rows, tk = vt.shape[1:]
    n_pairs = rows // (2 * LANES)
    tq = ATT_Q_TILE
    nq = seq_len // tq
    return pl.pallas_call(
        _attn_kernel,
        grid=(batch, n_pairs, nq),
        in_specs=[pl.BlockSpec((tq, 2 * LANES), lambda b, p, i: (b * nq + i, p)),
                  pl.BlockSpec((seq_len, 2 * LANES), lambda b, p, i: (b, p)),
                  pl.BlockSpec((None, nkt, 2 * LANES, tk), lambda b, p, i: (b, 0, p, 0))],
        out_specs=pl.BlockSpec((tq, PAIR_W), lambda b, p, i: (b * nq + i, p)),
        out_shape=jax.ShapeDtypeStruct((m, n_pairs * PAIR_W), BF16),
        compiler_params=_params("parallel", "parallel", "arbitrary"),
        name="mla_attention",
    )(q, k, vt)


def _pad_cols(w, n):
    return jnp.pad(w, ((0, 0), (0, n - w.shape[1])))


def _pad_rows(w, n):
    return jnp.pad(w, ((0, n - w.shape[0]), (0, 0)))


def _rows(vectors, d):
    rows = jnp.stack([v.reshape(d).astype(F32) for v in vectors])
    return _pad_rows(rows, -(-rows.shape[0] // SUBLANES) * SUBLANES)


def _rotate_half_cols(w):
    half = w.shape[1] // 2
    return jnp.concatenate([-w[:, half:], w[:, :half]], axis=1)


def _rope_tables(seq_len):
    inv = 1.0 / (ROPE_THETA ** (jnp.arange(0, QK_ROPE, 2, dtype=F32) / QK_ROPE))
    ang = jnp.arange(seq_len, dtype=F32)[:, None] * inv[None, :]
    cos, sin = jnp.cos(ang), jnp.sin(ang)
    pad = jnp.zeros((seq_len, QK_NOPE), F32)
    k_tab = jnp.concatenate([pad, cos, cos, cos, cos, pad, sin, sin, sin, sin], axis=1)
    q_tab = jnp.concatenate([jnp.ones((seq_len, QK_NOPE), F32), cos, cos, sin, sin], axis=1)
    return k_tab, q_tab


def kernel(x, norm_g, ffn_w_in, ffn_conv_w, ffn_conv_b, ffn_w_out, a_mu, a_w_rkv, a_w0, a_w1, a_w2, a_a0, a_a1, a_a2, a_g1, a_g2, a_k_k, a_k_a, a_r_k, a_lnx_w, a_lnx_b, a_w_o, kv_norm_g, kv_w_down, kv_a_norm_g, kv_w_up, q_w_down, q_norm_g, q_w_up, o_w):
    batch, seq_len, d = x.shape
    n_heads = d // HEAD
    n_a = a_mu.shape[0]
    depth = norm_g.shape[0]
    assert batch % SCAN_BATCH == 0 and seq_len % ATT_Q_TILE == 0 and ATT_Q_TILE % ATT_TILE == 0, (batch, seq_len)
    assert seq_len % ROW_TILE == 0 and seq_len % FFN_ROW_TILE == 0 and d % PAIR_W == 0, (seq_len, d)
    assert ffn_conv_b.shape[1] % (FFN_COL_BLOCK) == 0 and KV_LORA + QK_ROPE == kv_w_down.shape[1]
    xf = x.reshape(batch * seq_len, d)

    head_of = jnp.arange(d) // HEAD
    e = _bf(head_of[:, None] == jnp.arange(LANES)[None, :])
    et = e.T
    pair_of = jnp.arange(PAIR_W) // HEAD
    bd = _bf(pair_of[:, None] == pair_of[None, :])
    k_tab, q_tab = _rope_tables(seq_len)

    kq = vq = None
    for layer in range(depth):
        gl = norm_g[layer].astype(F32)
        if layer < n_a:
            i = layer
            vec = _rows([gl[0]] + [a_mu[i, n] for n in range(6)]
                        + [a_w0[i], a_a0[i], a_k_k[i], a_k_a[i]], d)
            lora = LANES
            gate_lora = 2 * LANES
            r, k, v, a, b, g, lw = _rwkv_prep(
                xf, seq_len, vec, _bf(a_w_rkv[i]),
                _bf(_pad_cols(a_w1[i], lora)), _bf(_pad_rows(a_w2[i], lora)),
                _bf(_pad_cols(a_a1[i], lora)), _bf(_pad_rows(a_a2[i], lora)),
                _bf(_pad_cols(a_g1[i], gate_lora)), _bf(_pad_rows(a_g2[i], gate_lora)), e, et)
            svec = _rows([a_lnx_w[i], a_lnx_b[i], a_r_k[i]], d)
            mix = _rwkv_scan(r, k, v, a, b, g, lw, svec, bd, batch, seq_len)
            w_out_proj = _bf(a_w_o[i])
        else:
            if layer == n_a:
                wd = kv_w_down[:, :KV_LORA]
                wr = kv_w_down[:, KV_LORA:]
                up = kv_w_up.reshape(KV_LORA, n_heads, 2 * HEAD)
                wk = _pad_cols(up[:, :, :QK_NOPE].reshape(KV_LORA * n_heads, QK_NOPE), LANES)
                wk = wk.reshape(KV_LORA, n_heads * LANES)
                wvt = _pad_cols(up[:, :, QK_NOPE:].reshape(KV_LORA * n_heads, HEAD), LANES)
                wvt = wvt.reshape(KV_LORA, n_heads * LANES).T
                nope_pad = jnp.zeros((d, QK_NOPE), F32)
                wr_rot = _rotate_half_cols(wr)
                wr2 = jnp.concatenate([nope_pad, wr, wr, nope_pad, wr_rot, wr_rot], axis=1)
                kq, vq = _kv_prep(
                    xf, batch, seq_len, kv_norm_g.reshape(1, d), kv_a_norm_g.reshape(1, KV_LORA),
                    _bf(wd), _bf(wr2), k_tab, _bf(wk), _bf(wvt))
            j = layer - n_a
            qup = q_w_up[j].reshape(-1, n_heads, QK_NOPE + QK_ROPE)
            q_lora = qup.shape[0]
            wa = jnp.concatenate([qup, -qup[:, :, QK_NOPE + QK_ROPE // 2:],
                                  qup[:, :, QK_NOPE:QK_NOPE + QK_ROPE // 2]], axis=-1).reshape(q_lora, n_heads * LANES)
            q = _q_prep(xf, seq_len, math.log2(math.e) / math.sqrt(QK_NOPE + QK_ROPE), gl[0:1], q_norm_g[j].reshape(1, -1),
                        _bf(q_w_down[j]), _bf(wa), q_tab)
            mix = _attention(q, kq, vq, batch, seq_len)
            w_out_proj = _bf(o_w[j])
        d_ff = ffn_conv_b.shape[1]
        conv = _rows([ffn_conv_w[layer, 0], ffn_conv_w[layer, 1], ffn_conv_w[layer, 2], ffn_conv_b[layer]], d_ff)
        xf = _mix_ffn(xf, mix, seq_len, w_out_proj, _rows([gl[1], gl[2], gl[3]], d),
                      _bf(ffn_w_in[layer]), conv, _bf(ffn_w_out[layer]))
    return xf.reshape(batch, seq_len, d)
```

```python
import functools
import math

import jax
import jax.numpy as jnp
from jax import lax
from jax.experimental import pallas as pl
from jax.experimental.pallas import tpu as pltpu

F32 = jnp.float32
BF16 = jnp.bfloat16

HEAD = 64
LNX_EPS = 64e-5
NORM_EPS = 1e-6
QK_NOPE = 64
QK_ROPE = 32
KV_LORA = 256
ROPE_THETA = 10000.0
MASK_CHUNK = 64
NEG_INF = -1e30

LANES = 128
SUBLANES = 8
VMEM_LIMIT = 48 * 1024 * 1024

SCAN_CHUNK = 64
SCAN_BATCH = 4
ROW_TILE = 512
FFN_ROW_TILE = 512
FFN_COL_BLOCK = 256
FFN_LOOKAHEAD = 2
FFN_OUT_GROUP = 6
FFN_PROLOGUE_CHUNKS = 4
ATT_TILE = 1024
ATT_Q_TILE = 2048
ATT_Q_BLOCK = 256


def _bf(x):
    return x.astype(BF16)


def _mm(a, b):
    return jnp.dot(a, b, preferred_element_type=F32)


def _mm_nt(a, b):
    return lax.dot_general(a, b, (((1,), (1,)), ((), ())), preferred_element_type=F32)


def _mm_tn(a, b):
    return lax.dot_general(a, b, (((0,), (0,)), ((), ())), preferred_element_type=F32)


def _rms(x, g):
    return x * lax.rsqrt(jnp.mean(x * x, axis=-1, keepdims=True) + NORM_EPS) * g


def _sigmoid(x):
    return 1.0 / (1.0 + jnp.exp2(x * -math.log2(math.e)))


def _params(*sem):
    return pltpu.CompilerParams(dimension_semantics=sem, vmem_limit_bytes=VMEM_LIMIT)


def _full(shape):
    nd = len(shape)
    return pl.BlockSpec(shape, lambda *_: (0,) * nd)


def _resident(shape):
    nd = len(shape)
    return pl.BlockSpec(shape, lambda *_: (0,) * nd, pipeline_mode=pl.Buffered(1))


def _rwkv_prep_kernel(seq_len, x_ref, xh_ref, vec_ref, wrkv_ref, w1_ref, w2_ref, a1_ref, a2_ref,
                      g1_ref, g2_ref, e_ref, et_ref,
                      r_ref, k_ref, v_ref, a_ref, b_ref, g_ref, lw_ref, h_ref, xx_ref):
    tm = x_ref.shape[0]
    gn = vec_ref[0:1, :]
    h = _rms(x_ref[...], gn)
    h_halo = _rms(xh_ref[SUBLANES - 1:SUBLANES, :], gn)
    at_start = (pl.program_id(0) * tm) % seq_len == 0
    h_halo = jnp.where(at_start, 0.0, h_halo)
    row = lax.broadcasted_iota(jnp.int32, h.shape, 0)
    h_prev = jnp.where(row == 0, h_halo, pltpu.roll(h, 1, 0))
    h_ref[...] = h
    xx_ref[...] = h_prev - h

    def mix(i):
        return _bf(h_ref[...] + xx_ref[...] * vec_ref[1 + i:2 + i, :])

    r = _mm(mix(0), wrkv_ref[0])
    k = _mm(mix(1), wrkv_ref[1])
    v = _mm(mix(2), wrkv_ref[2])
    zw = vec_ref[7:8, :] + _mm(_bf(jnp.tanh(_mm(mix(3), w1_ref[...]))), w2_ref[...])
    lw = -math.exp(-0.5) * _sigmoid(zw)
    a = _sigmoid(vec_ref[8:9, :] + _mm(_bf(_mm(mix(4), a1_ref[...])), a2_ref[...]))
    g = _mm(_bf(_sigmoid(_mm(mix(5), g1_ref[...]))), g2_ref[...])
    kk = k * vec_ref[9:10, :]
    ss = _mm(_bf(_mm(_bf(kk * kk), e_ref[...])), et_ref[...])
    kk = kk * lax.rsqrt(jnp.maximum(ss, 1e-24))
    k = k * (1.0 + (a - 1.0) * vec_ref[10:11, :])
    r_ref[...] = _bf(r)
    k_ref[...] = _bf(k)
    v_ref[...] = _bf(v)
    a_ref[...] = _bf(-kk)
    b_ref[...] = _bf(kk * a)
    g_ref[...] = _bf(g)
    lw_ref[...] = lw


def _rwkv_prep(x, seq_len, vec, wrkv, w1, w2, a1, a2, g1, g2, e, et):
    m, d = x.shape
    tm = ROW_TILE
    halo = tm // SUBLANES
    row_spec = pl.BlockSpec((tm, d), lambda i: (i, 0))
    out_bf = jax.ShapeDtypeStruct((m, d), BF16)
    return pl.pallas_call(
        functools.partial(_rwkv_prep_kernel, seq_len),
        grid=(m // tm,),
        in_specs=[row_spec,
                  pl.BlockSpec((SUBLANES, d), lambda i: (jnp.maximum(i * halo - 1, 0), 0)),
                  _full(vec.shape), _full(wrkv.shape), _full(w1.shape), _full(w2.shape),
                  _full(a1.shape), _full(a2.shape), _full(g1.shape), _full(g2.shape),
                  _full(e.shape), _full(et.shape)],
        out_specs=[row_spec] * 7,
        out_shape=[out_bf] * 6 + [jax.ShapeDtypeStruct((m, d), F32)],
        scratch_shapes=[pltpu.VMEM((tm, d), F32), pltpu.VMEM((tm, d), F32)],
        compiler_params=_params("parallel"),
        name="rwkv_prep",
    )(x, x, vec, wrkv, w1, w2, a1, a2, g1, g2, e, et)


PAIR_W = 2 * HEAD
HEAD_LOG2 = HEAD.bit_length() - 1
INV_BASE_LOG2 = 3


def _unit_lower_inverse(nms, row, col):
    diff = row ^ col
    eye = jnp.where(row == col, 1.0, 0.0)
    n8 = [jnp.where((diff >> INV_BASE_LOG2) == 0, nm, 0.0) for nm in nms]
    n8b = [_bf(n) for n in n8]
    n8_2 = [_mm(n, n) for n in n8b]
    n8_2b = [_bf(n) for n in n8_2]
    size = nms[0].shape[0]
    n43 = [_mm(jnp.concatenate([n2, n], axis=0), n2) for n, n2 in zip(n8b, n8_2b)]
    xs = [eye + n + n2 + m[size:] for n, n2, m in zip(n8, n8_2, n43)]
    xs = [x + _mm(_bf(x), _bf(m[:size])) for x, m in zip(xs, n43)]
    for shift in range(INV_BASE_LOG2, HEAD_LOG2):
        s = 1 << shift
        starts = range(0, size, 2 * s)

        def lower(m):
            return jnp.concatenate([m[b + s:b + 2 * s] for b in starts], axis=0)

        offs = [_bf(jnp.where((diff >> shift) == 1, nm, 0.0)) for nm in nms]
        xbs = [_bf(x) for x in xs]
        ts = [_bf(_mm(_bf(lower(x)), off)) for x, off in zip(xs, offs)]
        upd = [_mm(t, xb) for t, xb in zip(ts, xbs)]
        xs = [jnp.concatenate([piece for i, b in enumerate(starts)
                               for piece in (x[b:b + s], x[b + s:b + 2 * s] + u[i * s:(i + 1) * s])], axis=0)
              for x, u in zip(xs, upd)]
    return xs


def _rwkv_scan_kernel(r_ref, k_ref, v_ref, a_ref, b_ref, g_ref, lw_ref, vec_ref, bd_ref,
                      o_ref, s_ref):
    nb, chunk, d = lw_ref.shape

    @pl.when(pl.program_id(1) == 0)
    def _():
        s_ref[...] = jnp.zeros_like(s_ref)

    trow = lax.broadcasted_iota(jnp.int32, (chunk, chunk), 0)
    tcol = lax.broadcasted_iota(jnp.int32, (chunk, chunk), 1)
    tri = _bf(jnp.where(trow >= tcol, 1.0, 0.0))

    def decayed(j):
        lw = lw_ref[j]
        lw_hi = _bf(lw)
        lw_lo = _bf(lw - lw_hi.astype(F32))
        c = _mm(tri, lw_hi) + _mm(tri, lw_lo)
        c_last = c[chunk - 1:chunk, :]
        r = r_ref[j].astype(F32)
        k = k_ref[j].astype(F32)
        a = a_ref[j].astype(F32)
        b = b_ref[j].astype(F32)
        e_inv = jnp.exp(-c)
        e_rem = jnp.exp(c_last - c)
        at = a * jnp.exp(c - lw)
        rt = r * jnp.exp(c)
        return dict(r=r, k=k, at=at, rt=rt, bt=b * e_inv, kt=k * e_inv, bh=_bf(b * e_rem), kh=_bf(k * e_rem),
                    g_last=jnp.exp(c_last), atb=_bf(at), rtb=_bf(rt))

    rows = [decayed(j) for j in range(nb)]

    row = lax.broadcasted_iota(jnp.int32, (PAIR_W, PAIR_W), 0)
    col = lax.broadcasted_iota(jnp.int32, (PAIR_W, PAIR_W), 1)
    strict = (row & (HEAD - 1)) > (col & (HEAD - 1))
    incl = (row & (HEAD - 1)) >= (col & (HEAD - 1))
    same_head = (row >> HEAD_LOG2) == (col >> HEAD_LOG2)
    first_half = lax.broadcasted_iota(jnp.int32, (chunk, PAIR_W), 1) < HEAD
    bd = bd_ref[...]

    def stack(x):
        return jnp.concatenate([_bf(jnp.where(first_half, x, 0.0)), _bf(jnp.where(first_half, 0.0, x))], axis=0)

    n_pairs = d // PAIR_W
    items = [(j, slice(i * PAIR_W, (i + 1) * PAIR_W)) for j in range(nb) for i in range(n_pairs)]
    n_items = len(items)

    def op(name):
        return [rows[j][name][:, sl] for j, sl in items]

    lhs = [jnp.concatenate([stack(x), stack(y_)], axis=0) for x, y_ in zip(op("at"), op("rt"))]
    rhs = [jnp.concatenate([stack(x), stack(y_)], axis=0) for x, y_ in zip(op("bt"), op("kt"))]
    p = [_mm_nt(l, r_) for l, r_ in zip(lhs, rhs)]
    n_ab = [jnp.where(strict, x[:PAIR_W, :PAIR_W], 0.0) for x in p]
    a_ak = [_bf(jnp.where(strict, x[:PAIR_W, PAIR_W:], 0.0)) for x in p]
    aa = [jnp.concatenate([_bf(jnp.where(incl, x[PAIR_W:, :PAIR_W], 0.0)),
                           _bf(jnp.where(incl, x[PAIR_W:, PAIR_W:], 0.0))], axis=1) for x in p]
    tinv = [_bf(x) for x in _unit_lower_inverse(n_ab, row, col)]
    tt = [jnp.concatenate([t, _bf(_mm(t, ak))], axis=1) for t, ak in zip(tinv, a_ak)]

    s0 = [s_ref[i] for i in range(n_items)]
    vp = [v_ref[j, :, sl] for j, sl in items]
    qq = [_mm_nt(jnp.concatenate([x, y_], axis=0), _bf(s)) for x, y_, s in zip(op("atb"), op("rtb"), s0)]
    qa = [_bf(x[:chunk]) for x in qq]
    qr = [x[chunk:] for x in qq]
    u_st = [_mm(tt[i], jnp.concatenate([qa[i], qa[i], vp[i], vp[i]], axis=0)) for i in range(n_items)]
    y_st = [jnp.concatenate([qr[i], qr[i]], axis=0)
            + _mm(aa[i], jnp.concatenate([_bf(u_st[i]), vp[i], vp[i]], axis=0)) for i in range(n_items)]
    u = [jnp.where(first_half, x[:chunk], x[chunk:]) for x in u_st]
    y = [jnp.where(first_half, x[:chunk], x[chunk:]) for x in y_st]
    ds = [_mm_tn(jnp.concatenate([_bf(u[i]), vp[i]], axis=0), jnp.concatenate([x, y_], axis=0))
          for i, (x, y_) in enumerate(zip(op("bh"), op("kh")))]
    for i, g_last in enumerate(op("g_last")):
        s_ref[i] = s0[i] * g_last + jnp.where(same_head, ds[i], 0.0)

    def head_sums(xs):
        tot = _mm(jnp.concatenate([_bf(x) for x in xs], axis=0), bd)
        return [tot[i * chunk:(i + 1) * chunk] for i in range(n_items)]

    mean = [x * (1.0 / HEAD) for x in head_sums(y)]
    dev = [x - m_ for x, m_ in zip(y, mean)]
    var = [x * (1.0 / HEAD) for x in head_sums([x * x for x in dev])]
    bonus = head_sums([r_ * k_ * vec_ref[2:3, sl] for r_, k_, (_, sl) in zip(op("r"), op("k"), items)])
    for i, (j, sl) in enumerate(items):
        yn = dev[i] * lax.rsqrt(var[i] + LNX_EPS) * vec_ref[0:1, sl] + vec_ref[1:2, sl]
        o_ref[j, :, sl] = _bf((yn + bonus[i] * vp[i].astype(F32)) * g_ref[j, :, sl].astype(F32))


def _rwkv_scan(r, k, v, a, b, g, lw, vec, bd, batch, seq_len):
    m, d = lw.shape
    nb = SCAN_BATCH
    blk = pl.BlockSpec((nb, SCAN_CHUNK, d), lambda bi, ci: (bi, ci, 0))
    seq = [x.reshape(batch, seq_len, d) for x in (r, k, v, a, b, g, lw)]
    out = pl.pallas_call(
        _rwkv_scan_kernel,
        grid=(batch // nb, seq_len // SCAN_CHUNK),
        in_specs=[blk] * 7 + [_full(vec.shape), _full(bd.shape)],
        out_specs=blk,
        out_shape=jax.ShapeDtypeStruct((batch, seq_len, d), BF16),
        scratch_shapes=[pltpu.VMEM((nb * d // PAIR_W, PAIR_W, PAIR_W), F32)],
        compiler_params=_params("parallel", "arbitrary"),
        name="rwkv_scan",
    )(*seq, vec, bd)
    return out.reshape(m, d)


def _gelu_times(x, y):
    k1 = -2.0 * math.sqrt(2.0 / math.pi) * math.log2(math.e)
    e = jnp.exp2(x * (k1 + (k1 * 0.044715) * (x * x)))
    return (x * y) / (1.0 + e)


def _mix_ffn_kernel(tiles_per_seq, x_ref, o_ref, wp_ref, vec_ref, win_ref, cw_ref, wo_ref,
                    out_ref, tail_ref):
    tm = x_ref.shape[0]
    fc = FFN_COL_BLOCK
    rc = tm // FFN_PROLOGUE_CHUNKS
    x1s, xns = [], []
    for i in range(FFN_PROLOGUE_CHUNKS):
        rs = slice(i * rc, (i + 1) * rc)
        x1s.append(x_ref[rs, :] + _rms(_mm(o_ref[rs, :], wp_ref[...]), vec_ref[0:1, :]))
        xns.append(_bf(_rms(x1s[-1], vec_ref[1:2, :])))
    x1 = jnp.concatenate(x1s, axis=0)
    xn = jnp.concatenate(xns, axis=0)

    @pl.when(pl.program_id(0) % tiles_per_seq == 0)
    def _():
        tail_ref[...] = jnp.zeros_like(tail_ref)

    row = lax.broadcasted_iota(jnp.int32, (tm, fc), 0)
    acc = jnp.zeros(x1.shape, F32)
    f = win_ref.shape[1] // 2
    n_blocks = f // fc

    def gate_up(c):
        return _mm(xn, win_ref[:, c * fc:(c + 1) * fc]), _mm(xn, win_ref[:, f + c * fc:f + (c + 1) * fc])

    ahead = [gate_up(c) for c in range(min(FFN_LOOKAHEAD, n_blocks))]
    hidden = []
    for c in range(n_blocks):
        cs = slice(c * fc, (c + 1) * fc)
        gate, up = ahead.pop(0)
        if c + FFN_LOOKAHEAD < n_blocks:
            ahead.append(gate_up(c + FFN_LOOKAHEAD))
        t1 = tail_ref[SUBLANES - 1:SUBLANES, cs]
        t2 = tail_ref[SUBLANES - 2:SUBLANES - 1, cs]
        prev1 = jnp.where(row == 0, t1, pltpu.roll(gate, 1, 0))
        prev2 = jnp.where(row == 0, t2, jnp.where(row == 1, t1, pltpu.roll(gate, 2, 0)))
        tail_ref[:, cs] = gate[tm - SUBLANES:, :]
        gc = cw_ref[3:4, cs] + prev2 * cw_ref[0:1, cs] + prev1 * cw_ref[1:2, cs] + gate * cw_ref[2:3, cs]
        hidden.append(_bf(_gelu_times(gc, up)))
        if len(hidden) == FFN_OUT_GROUP or c == n_blocks - 1:
            lo = (c + 1 - len(hidden)) * fc
            acc = acc + _mm(jnp.concatenate(hidden, axis=1), wo_ref[lo:(c + 1) * fc, :])
            hidden = []
    out_ref[...] = x1 + _rms(acc, vec_ref[2:3, :])


def _mix_ffn(x, o, seq_len, w_proj, vec, w_in, conv, w_out):
    m, d = x.shape
    tm = FFN_ROW_TILE
    f = w_out.shape[0]
    return pl.pallas_call(
        functools.partial(_mix_ffn_kernel, seq_len // tm),
        grid=(m // tm,),
        in_specs=[pl.BlockSpec((tm, d), lambda i: (i, 0)),
                  pl.BlockSpec((tm, o.shape[1]), lambda i: (i, 0)),
                  _resident(w_proj.shape), _resident(vec.shape), _resident(w_in.shape),
                  _resident(conv.shape), _resident(w_out.shape)],
        out_specs=pl.BlockSpec((tm, d), lambda i: (i, 0)),
        out_shape=jax.ShapeDtypeStruct((m, d), F32),
        scratch_shapes=[pltpu.VMEM((SUBLANES, f), F32)],
        compiler_params=_params("arbitrary"),
        name="mix_ffn",
    )(x, o, w_proj, vec, w_in, conv, w_out)


def _kv_prep_kernel(x_ref, g_ref, ga_ref, wd_ref, wr_ref, cs_ref, wk_ref, wvt_ref, k_ref, vt_ref):
    xn = _bf(_rms(x_ref[...], g_ref[...]))
    ckv = _bf(_rms(_mm(xn, wd_ref[...]), ga_ref[...]))
    kr2 = _mm(xn, wr_ref[...]) * cs_ref[...]
    kr = kr2[:, :LANES] + kr2[:, LANES:]
    kn = _mm(ckv, wk_ref[...])
    for h in range(kn.shape[1] // LANES):
        sl = slice(h * LANES, (h + 1) * LANES)
        k_ref[:, sl] = _bf(kn[:, sl] + kr)
    vt = _mm_nt(wvt_ref[...], ckv)
    ones_row = (lax.broadcasted_iota(jnp.int32, vt.shape, 0) & (LANES - 1)) == HEAD
    vt_ref[...] = _bf(jnp.where(ones_row, 1.0, vt))


def _kv_prep(x, batch, seq_len, g, ga, wd, wr, cs, wk, wvt):
    m, d = x.shape
    tm = ATT_TILE
    nt = seq_len // tm
    return pl.pallas_call(
        _kv_prep_kernel,
        grid=(m // tm,),
        in_specs=[pl.BlockSpec((tm, d), lambda i: (i, 0)),
                  _full(g.shape), _full(ga.shape), _full(wd.shape), _full(wr.shape),
                  pl.BlockSpec((tm, 2 * LANES), lambda i: (i % nt, 0)),
                  _full(wk.shape), _full(wvt.shape)],
        out_specs=[pl.BlockSpec((tm, wk.shape[1]), lambda i: (i, 0)),
                   pl.BlockSpec((None, None, wvt.shape[0], tm), lambda i: (i // nt, i % nt, 0, 0))],
        out_shape=[jax.ShapeDtypeStruct((m, wk.shape[1]), BF16),
                   jax.ShapeDtypeStruct((batch, nt, wvt.shape[0], tm), BF16)],
        compiler_params=_params("parallel"),
        name="mla_kv_prep",
    )(x, g, ga, wd, wr, cs, wk, wvt)


def _q_prep_kernel(scale, x_ref, g_ref, gq_ref, wd_ref, wa_ref, tab_ref, q_ref):
    xn = _bf(_rms(x_ref[...], g_ref[...]))
    cq = _bf(_rms(_mm(xn, wd_ref[...]), gq_ref[...]))
    qa = _mm(cq, wa_ref[...])
    tab = tab_ref[...] * scale
    for h in range(qa.shape[1] // LANES):
        sl = slice(h * LANES, (h + 1) * LANES)
        q_ref[:, sl] = _bf(qa[:, sl] * tab)


def _q_prep(x, seq_len, scale, g, gq, wd, wa, tab):
    m, d = x.shape
    tm = ROW_TILE
    nt = seq_len // tm
    return pl.pallas_call(
        functools.partial(_q_prep_kernel, scale),
        grid=(m // tm,),
        in_specs=[pl.BlockSpec((tm, d), lambda i: (i, 0)),
                  _full(g.shape), _full(gq.shape), _full(wd.shape), _full(wa.shape),
                  pl.BlockSpec((tm, LANES), lambda i: (i % nt, 0))],
        out_specs=pl.BlockSpec((tm, wa.shape[1]), lambda i: (i, 0)),
        out_shape=jax.ShapeDtypeStruct((m, wa.shape[1]), BF16),
        compiler_params=_params("parallel"),
        name="mla_q_prep",
    )(x, g, gq, wd, wa, tab)


def _attn_kernel(q_ref, k_ref, vt_ref, o_ref):
    tq = q_ref.shape[0]
    tk = vt_ref.shape[-1]
    tc = ATT_Q_BLOCK
    n_blocks = tq // tc
    tiles_per_q = tq // tk
    qi = pl.program_id(2)
    streams = [(h, c) for h in range(2) for c in range(n_blocks)]
    hsl = [slice(h * LANES, (h + 1) * LANES) for h in range(2)]
    qs = [q_ref[c * tc:(c + 1) * tc, hsl[h]] for h, c in streams]

    def step(s, vt, m_i, acc):
        m_new = jnp.maximum(m_i, jnp.max(s, axis=0, keepdims=True))
        alpha = jnp.exp2(m_i - m_new)
        p = jnp.exp2(s - m_new)
        return m_new, alpha * acc + _mm(vt, _bf(p))

    def body(j, carry):
        off = pl.multiple_of(j * tk, tk)
        kt = [k_ref[pl.ds(off, tk), sl] for sl in hsl]
        ss = [_mm_nt(kt[h], q) for (h, _), q in zip(streams, qs)]
        return tuple(step(s, vt_ref[j, hsl[h], :], *mc) for (h, _), s, mc in zip(streams, ss, carry))

    init = tuple((jnp.full((1, tc), NEG_INF, F32), jnp.zeros((LANES, tc), F32)) for _ in streams)
    carry = lax.fori_loop(0, qi * tiles_per_q, body, init)

    off = pl.multiple_of(qi * tq, tq)
    outs = []
    kpos = lax.broadcasted_iota(jnp.int32, (tc, tc), 0) // MASK_CHUNK
    qpos = lax.broadcasted_iota(jnp.int32, (tc, tc), 1) // MASK_CHUNK
    visible = kpos <= qpos
    ss = [_mm_nt(k_ref[pl.ds(off, (c + 1) * tc), hsl[h]], q) for (h, c), q in zip(streams, qs)]
    for (h, c), s, mc in zip(streams, ss, carry):
        nk = (c + 1) * tc
        s_last = jnp.where(visible, s[nk - tc:], NEG_INF)
        s = s_last if c == 0 else jnp.concatenate([s[:nk - tc], s_last], axis=0)
        vt = [vt_ref[qi * tiles_per_q + t, hsl[h], :min(tk, nk - t * tk)] for t in range(-(-nk // tk))]
        _, acc = step(s, vt[0] if len(vt) == 1 else jnp.concatenate(vt, axis=1), *mc)
        outs.append(acc[:HEAD] / acc[HEAD:HEAD + 1])
    o_t = jnp.concatenate([jnp.concatenate(outs[h * n_blocks:(h + 1) * n_blocks], axis=1)
                           for h in range(2)], axis=0)
    o_ref[...] = _bf(o_t.T)


def _attention(q, k, vt, batch, seq_len):
    m = q.shape[0]
    nkt, rows, tk = vt.shape[1:]
    n_pairs = rows // (2 * LANES)
    tq = ATT_Q_TILE
    nq = seq_len // tq
    return pl.pallas_call(
        _attn_kernel,
        grid=(batch, n_pairs, nq),
        in_specs=[pl.BlockSpec((tq, 2 * LANES), lambda b, p, i: (b * nq + i, p)),
                  pl.BlockSpec((seq_len, 2 * LANES), lambda b, p, i: (b, p)),
                  pl.BlockSpec((None, nkt, 2 * LANES, tk), lambda b, p, i: (b, 0, p, 0))],
        out_specs=pl.BlockSpec((tq, PAIR_W), lambda b, p, i: (b * nq + i, p)),
        out_shape=jax.ShapeDtypeStruct((m, n_pairs * PAIR_W), BF16),
        compiler_params=_params("parallel", "parallel", "arbitrary"),
        name="mla_attention",
    )(q, k, vt)


def _pad_cols(w, n):
    return jnp.pad(w, ((0, 0), (0, n - w.shape[1])))


def _pad_rows(w, n):
    return jnp.pad(w, ((0, n - w.shape[0]), (0, 0)))


def _rows(vectors, d):
    rows = jnp.stack([v.reshape(d).astype(F32) for v in vectors])
    return _pad_rows(rows, -(-rows.shape[0] // SUBLANES) * SUBLANES)


def _rotate_half_cols(w):
    half = w.shape[1] // 2
    return jnp.concatenate([-w[:, half:], w[:, :half]], axis=1)


def _rope_tables(seq_len):
    inv = 1.0 / (ROPE_THETA ** (jnp.arange(0, QK_ROPE, 2, dtype=F32) / QK_ROPE))
    ang = jnp.arange(seq_len, dtype=F32)[:, None] * inv[None, :]
    cos, sin = jnp.cos(ang), jnp.sin(ang)
    pad = jnp.zeros((seq_len, QK_NOPE), F32)
    k_tab = jnp.concatenate([pad, cos, cos, cos, cos, pad, sin, sin, sin, sin], axis=1)
    q_tab = jnp.concatenate([jnp.ones((seq_len, QK_NOPE), F32), cos, cos, sin, sin], axis=1)
    return k_tab, q_tab


def kernel(x, norm_g, ffn_w_in, ffn_conv_w, ffn_conv_b, ffn_w_out, a_mu, a_w_rkv, a_w0, a_w1, a_w2, a_a0, a_a1, a_a2, a_g1, a_g2, a_k_k, a_k_a, a_r_k, a_lnx_w, a_lnx_b, a_w_o, kv_norm_g, kv_w_down, kv_a_norm_g, kv_w_up, q_w_down, q_norm_g, q_w_up, o_w):
    batch, seq_len, d = x.shape
    n_heads = d // HEAD
    n_a = a_mu.shape[0]
    depth = norm_g.shape[0]
    assert batch % SCAN_BATCH == 0 and seq_len % ATT_Q_TILE == 0 and ATT_Q_TILE % ATT_TILE == 0, (batch, seq_len)
    assert seq_len % ROW_TILE == 0 and seq_len % FFN_ROW_TILE == 0 and d % PAIR_W == 0, (seq_len, d)
    assert ffn_conv_b.shape[1] % (FFN_COL_BLOCK) == 0 and KV_LORA + QK_ROPE == kv_w_down.shape[1]
    xf = x.reshape(batch * seq_len, d)

    head_of = jnp.arange(d) // HEAD
    e = _bf(head_of[:, None] == jnp.arange(LANES)[None, :])
    et = e.T
    pair_of = jnp.arange(PAIR_W) // HEAD
    bd = _bf(pair_of[:, None] == pair_of[None, :])
    k_tab, q_tab = _rope_tables(seq_len)

    kq = vq = None
    for layer in range(depth):
        gl = norm_g[layer].astype(F32)
        if layer < n_a:
            i = layer
            vec = _rows([gl[0]] + [a_mu[i, n] for n in range(6)]
                        + [a_w0[i], a_a0[i], a_k_k[i], a_k_a[i]], d)
            lora = LANES
            gate_lora = 2 * LANES
            r, k, v, a, b, g, lw = _rwkv_prep(
                xf, seq_len, vec, _bf(a_w_rkv[i]),
                _bf(_pad_cols(a_w1[i], lora)), _bf(_pad_rows(a_w2[i], lora)),
                _bf(_pad_cols(a_a1[i], lora)), _bf(_pad_rows(a_a2[i], lora)),
                _bf(_pad_cols(a_g1[i], gate_lora)), _bf(_pad_rows(a_g2[i], gate_lora)), e, et)
            svec = _rows([a_lnx_w[i], a_lnx_b[i], a_r_k[i]], d)
            mix = _rwkv_scan(r, k, v, a, b, g, lw, svec, bd, batch, seq_len)
            w_out_proj = _bf(a_w_o[i])
        else:
            if layer == n_a:
                wd = kv_w_down[:, :KV_LORA]
                wr = kv_w_down[:, KV_LORA:]
                up = kv_w_up.reshape(KV_LORA, n_heads, 2 * HEAD)
                wk = _pad_cols(up[:, :, :QK_NOPE].reshape(KV_LORA * n_heads, QK_NOPE), LANES)
                wk = wk.reshape(KV_LORA, n_heads * LANES)
                wvt = _pad_cols(up[:, :, QK_NOPE:].reshape(KV_LORA * n_heads, HEAD), LANES)
                wvt = wvt.reshape(KV_LORA, n_heads * LANES).T
                nope_pad = jnp.zeros((d, QK_NOPE), F32)
                wr_rot = _rotate_half_cols(wr)
                wr2 = jnp.concatenate([nope_pad, wr, wr, nope_pad, wr_rot, wr_rot], axis=1)
                kq, vq = _kv_prep(
                    xf, batch, seq_len, kv_norm_g.reshape(1, d), kv_a_norm_g.reshape(1, KV_LORA),
                    _bf(wd), _bf(wr2), k_tab, _bf(wk), _bf(wvt))
            j = layer - n_a
            qup = q_w_up[j].reshape(-1, n_heads, QK_NOPE + QK_ROPE)
            q_lora = qup.shape[0]
            wa = jnp.concatenate([qup, -qup[:, :, QK_NOPE + QK_ROPE // 2:],
                                  qup[:, :, QK_NOPE:QK_NOPE + QK_ROPE // 2]], axis=-1).reshape(q_lora, n_heads * LANES)
            q = _q_prep(xf, seq_len, math.log2(math.e) / math.sqrt(QK_NOPE + QK_ROPE), gl[0:1], q_norm_g[j].reshape(1, -1),
                        _bf(q_w_down[j]), _bf(wa), q_tab)
            mix = _attention(q, kq, vq, batch, seq_len)
            w_out_proj = _bf(o_w[j])
        d_ff = ffn_conv_b.shape[1]
        conv = _rows([ffn_conv_w[layer, 0], ffn_conv_w[layer, 1], ffn_conv_w[layer, 2], ffn_conv_b[layer]], d_ff)
        xf = _mix_ffn(xf, mix, seq_len, w_out_proj, _rows([gl[1], gl[2], gl[3]], d),
                      _bf(ffn_w_in[layer]), conv, _bf(ffn_w_out[layer]))
    return xf.reshape(batch, seq_len, d)
```
